```python
import jax, jax.numpy as jnp
from jax import lax
import numpy as np

D_MODEL = 1024
BATCH = 2
SEQ = 8192
DEPTH = 1
DEC_BATCH = 16
DEC_SEQ = 16
PAST_LEN = 1024

CHUNK = 64
MIX = D_MODEL
D_A = MIX // 2
D_B = MIX - D_A
H_A = 4
DK = D_A // H_A
DV = D_A // H_A
H_B = 4
C_B = D_B // H_B
MLP_CHUNK = 128
IN_WIDTH = 4 * D_A + 3 * D_B
EPS = 1e-6

kernel_name = "hgrn2_chunkmlp_hybrid_stream_step"


def rmsnorm(x, w):
    xf = x.astype(jnp.float32)
    y = xf * lax.rsqrt(jnp.mean(xf * xf, axis=-1, keepdims=True) + EPS)
    return (y * w.astype(jnp.float32)).astype(x.dtype)


def hgrn2_chunkwise(q, k, v, logf, s0, chunk):
    bsz, t, _ = q.shape
    n = t // chunk

    def heads(a, d):
        return a.reshape(bsz, n, chunk, H_A, d).transpose(1, 0, 3, 2, 4)

    qc, kc, gc, vc = heads(q, DK), heads(k, DK), heads(logf, DK), heads(v, DV)
    causal = jnp.tril(jnp.ones((chunk, chunk), dtype=bool))[:, :, None]

    def step(s, xs):
        qb, kb, vb, gb = xs
        b = jnp.cumsum(gb, axis=2)
        inter = jnp.einsum("bhtk,bhkv->bhtv", qb * jnp.exp(b), s)
        diff = b[:, :, :, None, :] - b[:, :, None, :, :]
        decay = jnp.exp(jnp.where(causal, diff, -jnp.inf))
        scores = jnp.einsum("bhtk,bhsk,bhtsk->bhts", qb, kb, decay)
        intra = jnp.einsum("bhts,bhsv->bhtv", scores, vb)
        b_last = b[:, :, -1, :]
        s_new = s * jnp.exp(b_last)[..., None] + jnp.einsum(
            "bhsk,bhsv->bhkv", kb * jnp.exp(b_last[:, :, None, :] - b), vb)
        return s_new, inter + intra

    s_fin, o = lax.scan(step, s0, (qc, kc, vc, gc))
    o = o.transpose(1, 0, 3, 2, 4).reshape(bsz, t, H_A * DV)
    return o, s_fin


def spatial_mix(v, w_s, b_s):
    bsz, t, _ = v.shape
    L = min(MLP_CHUNK, t)
    n = t // L
    w = jnp.tril(w_s[:, :L, :L].astype(jnp.float32))
    bias = b_s[:, :L].astype(jnp.float32).T
    vc = v.reshape(bsz, n, L, H_B, C_B)
    out = jnp.einsum("hts,bnshc->bnthc", w, vc) + bias[None, None, :, :, None]
    return out.reshape(bsz, t, D_B)


def mixer_layer(x, s0, norm_w, w_in, lb, g_norm_w, ln_v_w, ln_v_b, w_s, b_s, w_out):
    bsz, t, _ = x.shape
    h = rmsnorm(x, norm_w)
    z = jnp.einsum("btd,de->bte", h, w_in).astype(jnp.float32)
    q = z[..., 0:D_A]
    f_raw = z[..., D_A:2 * D_A]
    i = z[..., 2 * D_A:3 * D_A]
    g_a = z[..., 3 * D_A:4 * D_A]
    u = z[..., 4 * D_A:4 * D_A + D_B]
    v = z[..., 4 * D_A + D_B:4 * D_A + 2 * D_B]
    g_b = z[..., 4 * D_A + 2 * D_B:]

    f = lb + (1.0 - lb) * jax.nn.sigmoid(f_raw)
    o_a, s_fin = hgrn2_chunkwise(jax.nn.silu(q) * (DK ** -0.5), 1.0 - f, i, jnp.log(f),
                                 s0.astype(jnp.float32), min(CHUNK, t))
    o_a = o_a.reshape(bsz, t, H_A, DV)
    o_a = o_a * lax.rsqrt(jnp.mean(o_a * o_a, axis=-1, keepdims=True) + EPS)
    y_a = o_a.reshape(bsz, t, D_A) * g_norm_w.astype(jnp.float32) * jax.nn.silu(g_a)

    u = jax.nn.gelu(u, approximate=False)
    v = jax.nn.gelu(v, approximate=False)
    mu = jnp.mean(v, axis=-1, keepdims=True)
    var = jnp.mean(jnp.square(v - mu), axis=-1, keepdims=True)
    v = (v - mu) * lax.rsqrt(var + EPS) * ln_v_w.astype(jnp.float32) + ln_v_b.astype(jnp.float32)
    y_b = u * spatial_mix(v, w_s, b_s) * jax.nn.silu(g_b)

    y = jnp.concatenate([y_a, y_b], axis=-1).astype(x.dtype)
    out = x + jnp.einsum("bte,ed->btd", y, w_out).astype(x.dtype)
    return out, s_fin, v.astype(x.dtype)


def setup_inputs(seed: int = 0) -> dict:
    key = jax.random.key(seed)
    ks = jax.random.split(key, 14)
    f32 = jnp.float32
    return {
        "x_prompt": jax.random.normal(ks[0], (BATCH, SEQ, D_MODEL), f32),
        "x_sample": jax.random.normal(ks[1], (DEC_BATCH, DEC_SEQ, D_MODEL), f32),
        "state_hgrn": 0.5 * jax.random.normal(ks[2], (DEPTH, DEC_BATCH, H_A, DK, DV), f32),
        "norm_w": 1.0 + 0.02 * jax.random.normal(ks[3], (DEPTH, D_MODEL), f32),
        "w_in": jax.random.normal(ks[4], (DEPTH, D_MODEL, IN_WIDTH), f32) * D_MODEL ** -0.5,
        "lb_logits": 0.1 * jax.random.normal(ks[5], (DEPTH + 1, D_A), f32),
        "g_norm_w": 1.0 + 0.02 * jax.random.normal(ks[6], (DEPTH, D_A), f32),
        "ln_v_w": 1.0 + 0.02 * jax.random.normal(ks[7], (DEPTH, D_B), f32),
        "ln_v_b": 0.02 * jax.random.normal(ks[8], (DEPTH, D_B), f32),
        "w_s": jax.random.normal(ks[9], (DEPTH, H_B, MLP_CHUNK, MLP_CHUNK), f32) * MLP_CHUNK ** -0.5,
        "b_s": 1.0 + 0.1 * jax.random.normal(ks[10], (DEPTH, H_B, MLP_CHUNK), f32),
        "w_out": jax.random.normal(ks[11], (DEPTH, MIX, D_MODEL), f32) * MIX ** -0.5,
        "final_norm_w": 1.0 + 0.02 * jax.random.normal(ks[12], (D_MODEL,), f32),
    }


def reference(x_prompt, x_sample, state_hgrn, norm_w, w_in, lb_logits, g_norm_w, ln_v_w, ln_v_b,
              w_s, b_s, w_out, final_norm_w):
    lb_all = jnp.cumsum(jax.nn.softmax(lb_logits.astype(jnp.float32), axis=0), axis=0)
    hp, hs = x_prompt, x_sample
    sp_list, ss_list, v_list = [], [], []
    for l in range(DEPTH):
        params = (norm_w[l], w_in[l], lb_all[l], g_norm_w[l], ln_v_w[l], ln_v_b[l], w_s[l], b_s[l], w_out[l])
        s0_prompt = jnp.zeros((x_prompt.shape[0], H_A, DK, DV), jnp.float32)
        hp, s_p, _ = mixer_layer(hp, s0_prompt, *params)
        hs, s_s, v_s = mixer_layer(hs, state_hgrn[l], *params)
        sp_list.append(s_p)
        ss_list.append(s_s)
        v_list.append(v_s)
    y_prompt = rmsnorm(hp, final_norm_w)
    y_sample = rmsnorm(hs, final_norm_w)
    state_hgrn_prompt = jnp.stack(sp_list, axis=0)
    state_hgrn_sample = jnp.stack(ss_list, axis=0)
    mlp_v_sample = jnp.stack(v_list, axis=0)
    return (y_prompt, y_sample, state_hgrn_prompt, state_hgrn_sample, mlp_v_sample)
```

```python
import functools
import math

import jax
import jax.numpy as jnp
from jax import lax
from jax.experimental import pallas as pl
from jax.experimental.pallas import tpu as pltpu

D_MODEL = 1024
D_A = 512
D_B = 512
H_A = 4
DK = 128
DV = 128
H_B = 4
C_B = 128
MLP_CHUNK = 128
HGRN_CHUNK = 64
IN_WIDTH = 4 * D_A + 3 * D_B
EPS = 1e-6

PROMPT_BLOCK = 256
PROMPT_CHUNK = 128
VMEM_LIMIT_BYTES = 56 * 1024 * 1024

_NT = (((1,), (1,)), ((), ()))
_TN = (((0,), (0,)), ((), ()))
_F32 = jnp.float32
_BF16 = jnp.bfloat16


def _dot(a, b, dims=None):
    if dims is None:
        return jnp.dot(a, b, preferred_element_type=_F32)
    return lax.dot_general(a, b, dims, preferred_element_type=_F32)


def _rms(x, w):
    return x * lax.rsqrt(jnp.mean(x * x, axis=-1, keepdims=True) + EPS) * w


def _sigmoid(x):
    return 1.0 / (1.0 + jnp.exp(-x))


def _gelu_exact(x):
    return 0.5 * x * (1.0 + lax.erf(x * math.sqrt(0.5)))


def _same_chunk_causal(rows, chunk):
    shift = chunk.bit_length() - 1
    assert chunk == 1 << shift
    t = lax.broadcasted_iota(jnp.int32, (rows, rows), 0)
    s = lax.broadcasted_iota(jnp.int32, (rows, rows), 1)
    same = lax.shift_right_logical(t, shift) == lax.shift_right_logical(s, shift)
    return jnp.logical_and(same, s <= t)


def _layer_block(x, states_t, refs, *, chunk, mix_chunk, carry_state):
    (norm_w_ref, w_in_ref, lb_logits_ref, g_norm_w_ref, ln_v_w_ref, ln_v_b_ref,
     w_mix_ref, bias_mix_ref, w_out_ref, final_w_ref) = refs
    rows = x.shape[0]
    n_chunks = rows // chunk

    h = _rms(x, norm_w_ref[...]).astype(_BF16)

    def proj(k):
        return _dot(h, w_in_ref[:, k * D_A:(k + 1) * D_A])

    logits = lb_logits_ref[...]
    e = jnp.exp(logits - jnp.max(logits, axis=0, keepdims=True))
    lb = e[0:1, :] / jnp.sum(e, axis=0, keepdims=True)

    q = proj(0)
    qs = q * _sigmoid(q) * (DK ** -0.5)
    f = lb + (1.0 - lb) * _sigmoid(proj(1))
    kk = 1.0 - f
    g = jnp.log(f)
    iv = proj(2).astype(_BF16)

    blk_mask = _same_chunk_causal(rows, chunk)
    cum_mat = jnp.where(blk_mask, 1.0, 0.0).astype(_BF16)
    g_hi = g.astype(_BF16)
    g_lo = (g - g_hi.astype(_F32)).astype(_BF16)
    b = _dot(cum_mat, g_hi) + _dot(cum_mat, g_lo)

    q_dec = (qs * jnp.exp(b)).astype(_BF16)
    k_inv = kk * jnp.exp(-b)
    k_inv_bf = k_inv.astype(_BF16)

    new_states = [[None] * H_A for _ in range(n_chunks)]
    o_heads = []
    for hd in range(H_A):
        ln = slice(hd * DK, (hd + 1) * DK)
        sc = _dot(q_dec[:, ln], k_inv_bf[:, ln], _NT)
        sc = jnp.where(blk_mask, sc, 0.0).astype(_BF16)
        intra = _dot(sc, iv[:, ln])
        o_chunks = []
        s_t = states_t[0][hd]
        for c in range(n_chunks):
            rw = slice(c * chunk, (c + 1) * chunk)
            if not carry_state:
                s_t = states_t[c][hd]
            inter = _dot(q_dec[rw, ln], s_t.astype(_BF16), _NT)
            o_chunks.append(inter + intra[rw])
            last = (c + 1) * chunk - 1
            dec_last = jnp.exp(b[last:last + 1, ln])
            k_upd = (k_inv[rw, ln] * dec_last).astype(_BF16)
            s_t = s_t * dec_last + _dot(iv[rw, ln], k_upd, _TN)
            new_states[c][hd] = s_t
        o_heads.append(jnp.concatenate(o_chunks, axis=0) if n_chunks > 1 else o_chunks[0])

    ga = proj(3)
    y_a_heads = []
    for hd in range(H_A):
        o = o_heads[hd]
        o = o * lax.rsqrt(jnp.mean(o * o, axis=-1, keepdims=True) + EPS)
        y_a_heads.append(o)
    y_a = jnp.concatenate(y_a_heads, axis=1) * g_norm_w_ref[...] * (ga * _sigmoid(ga))

    u = _gelu_exact(proj(4))
    v = _gelu_exact(proj(5))
    mu = jnp.mean(v, axis=-1, keepdims=True)
    vc = v - mu
    var = jnp.mean(vc * vc, axis=-1, keepdims=True)
    v_norm = vc * lax.rsqrt(var + EPS) * ln_v_w_ref[...] + ln_v_b_ref[...]
    gb = proj(6)
    v_bf = v_norm.astype(_BF16)
    mix_mask = blk_mask if mix_chunk == chunk else _same_chunk_causal(rows, mix_chunk)
    mix_heads = []
    for hb in range(H_B):
        ln = slice(hb * C_B, (hb + 1) * C_B)
        w = jnp.where(mix_mask, w_mix_ref[hb], 0.0).astype(_BF16)
        mix_heads.append(_dot(w, v_bf[:, ln]))
    mix = jnp.concatenate(mix_heads, axis=1) + bias_mix_ref[...]
    y_b = u * mix * (gb * _sigmoid(gb))

    y = jnp.concatenate([y_a, y_b], axis=1).astype(_BF16)
    out = x + _dot(y, w_out_ref[...])
    return _rms(out, final_w_ref[...]), new_states, v_norm


def _prompt_kernel(x_ref, norm_w_ref, w_in_ref, lb_logits_ref, g_norm_w_ref, ln_v_w_ref,
                   ln_v_b_ref, w_mix_ref, bias_mix_ref, w_out_ref, final_w_ref,
                   y_ref, state_ref, state_t_scr):
    j = pl.program_id(1)

    @pl.when(j == 0)
    def _():
        state_t_scr[...] = jnp.zeros_like(state_t_scr)

    refs = (norm_w_ref, w_in_ref, lb_logits_ref, g_norm_w_ref, ln_v_w_ref, ln_v_b_ref,
            w_mix_ref, bias_mix_ref, w_out_ref, final_w_ref)
    states_t = [[state_t_scr[hd] for hd in range(H_A)]]
    y, new_states, _ = _layer_block(x_ref[0], states_t, refs, chunk=PROMPT_CHUNK,
                                    mix_chunk=MLP_CHUNK, carry_state=True)
    y_ref[0] = y
    for hd in range(H_A):
        state_t_scr[hd] = new_states[-1][hd]

    @pl.when(j == pl.num_programs(1) - 1)
    def _():
        for hd in range(H_A):
            state_ref[0, hd] = state_t_scr[hd].T


def _sample_kernel(x_ref, state_in_ref, norm_w_ref, w_in_ref, lb_logits_ref, g_norm_w_ref,
                   ln_v_w_ref, ln_v_b_ref, w_mix_ref, bias_mix_ref, w_out_ref, final_w_ref,
                   y_ref, state_out_ref, v_ref, *, n_streams, frames):
    refs = (norm_w_ref, w_in_ref, lb_logits_ref, g_norm_w_ref, ln_v_w_ref, ln_v_b_ref,
            w_mix_ref, bias_mix_ref, w_out_ref, final_w_ref)
    states_t = [[state_in_ref[s, hd].T for hd in range(H_A)] for s in range(n_streams)]
    y, new_states, v_norm = _layer_block(x_ref[...], states_t, refs, chunk=frames,
                                         mix_chunk=frames, carry_state=False)
    y_ref[...] = y
    v_ref[...] = v_norm
    for s in range(n_streams):
        for hd in range(H_A):
            state_out_ref[s, hd] = new_states[s][hd].T


def _full(shape):
    return pl.BlockSpec(shape, lambda *_: (0,) * len(shape))


def _mix_operands(w_s, b_s, rows, mix_chunk):
    reps = rows // mix_chunk
    w_rep = jnp.tile(w_s[:, :mix_chunk, :mix_chunk], (1, reps, reps))
    bias = jnp.tile(b_s[:, :mix_chunk].T, (reps, 1))
    bias = jnp.repeat(bias, C_B, axis=1)
    return w_rep, bias


def kernel(x_prompt, x_sample, state_hgrn, norm_w, w_in, lb_logits, g_norm_w, ln_v_w, ln_v_b, w_s, b_s, w_out, final_norm_w):
    batch, seq, _ = x_prompt.shape
    dec_batch, dec_seq, _ = x_sample.shape
    assert norm_w.shape[0] == 1, "single-layer trunk"
    assert seq % PROMPT_BLOCK == 0 and PROMPT_BLOCK % MLP_CHUNK == 0 and MLP_CHUNK % PROMPT_CHUNK == 0

    w_in_bf = w_in[0].astype(_BF16)
    w_out_bf = w_out[0].astype(_BF16)
    row = lambda a: a.reshape(1, -1)
    params = (row(norm_w[0]), w_in_bf, lb_logits, row(g_norm_w[0]), row(ln_v_w[0]), row(ln_v_b[0]))
    tail = (w_out_bf, row(final_norm_w))
    param_specs = [_full((1, D_MODEL)), _full((D_MODEL, IN_WIDTH)), _full(lb_logits.shape),
                   _full((1, D_A)), _full((1, D_B)), _full((1, D_B))]
    tail_specs = [_full((D_MODEL, D_MODEL)), _full((1, D_MODEL))]

    rp = PROMPT_BLOCK
    w_rep, bias = _mix_operands(w_s[0], b_s[0], rp, MLP_CHUNK)
    y_prompt, state_p = pl.pallas_call(
        _prompt_kernel,
        grid=(batch, seq // rp),
        in_specs=[pl.BlockSpec((1, rp, D_MODEL), lambda b, j: (b, j, 0))] + param_specs
                 + [_full((H_B, rp, rp)), _full((rp, D_B))] + tail_specs,
        out_specs=[pl.BlockSpec((1, rp, D_MODEL), lambda b, j: (b, j, 0)),
                   pl.BlockSpec((1, H_A, DK, DV), lambda b, j: (b, 0, 0, 0))],
        out_shape=[jax.ShapeDtypeStruct((batch, seq, D_MODEL), _F32),
                   jax.ShapeDtypeStruct((batch, H_A, DK, DV), _F32)],
        scratch_shapes=[pltpu.VMEM((H_A, DV, DK), _F32)],
        compiler_params=pltpu.CompilerParams(
            dimension_semantics=("arbitrary", "arbitrary"), vmem_limit_bytes=VMEM_LIMIT_BYTES),
        name="prompt_layer",
    )(x_prompt, *params, w_rep, bias, *tail)

    rs = dec_batch * dec_seq
    mix_len = min(MLP_CHUNK, dec_seq)
    assert dec_seq <= HGRN_CHUNK and dec_seq == mix_len and dec_seq & (dec_seq - 1) == 0
    w_rep_s, bias_s = _mix_operands(w_s[0], b_s[0], rs, mix_len)
    y_sample, state_s, v_sample = pl.pallas_call(
        functools.partial(_sample_kernel, n_streams=dec_batch, frames=dec_seq),
        grid=(1,),
        in_specs=[_full((rs, D_MODEL)), _full((dec_batch, H_A, DK, DV))] + param_specs
                 + [_full((H_B, rs, rs)), _full((rs, D_B))] + tail_specs,
        out_specs=[_full((rs, D_MODEL)), _full((dec_batch, H_A, DK, DV)), _full((rs, D_B))],
        out_shape=[jax.ShapeDtypeStruct((rs, D_MODEL), _F32),
                   jax.ShapeDtypeStruct((dec_batch, H_A, DK, DV), _F32),
                   jax.ShapeDtypeStruct((rs, D_B), _F32)],
        compiler_params=pltpu.CompilerParams(
            dimension_semantics=("arbitrary",), vmem_limit_bytes=VMEM_LIMIT_BYTES),
        name="sample_layer",
    )(x_sample.reshape(rs, D_MODEL), state_hgrn[0], *params, w_rep_s, bias_s, *tail)

    return (y_prompt,
            y_sample.reshape(dec_batch, dec_seq, D_MODEL),
            state_p[None],
            state_s[None],
            v_sample.reshape(1, dec_batch, dec_seq, D_B))
```

```python
import functools
import math

import jax
import jax.numpy as jnp
from jax import lax
from jax.experimental import pallas as pl
from jax.experimental.pallas import tpu as pltpu

D_MODEL = 1024
D_A = 512
D_B = 512
H_A = 4
DK = 128
DV = 128
H_B = 4
C_B = 128
MLP_CHUNK = 128
HGRN_CHUNK = 64
IN_WIDTH = 4 * D_A + 3 * D_B
EPS = 1e-6

PROMPT_BLOCK = 512
PROMPT_GROUP = 128
PROMPT_CHUNK = 128
VMEM_LIMIT_BYTES = 56 * 1024 * 1024

_NT = (((1,), (1,)), ((), ()))
_TN = (((0,), (0,)), ((), ()))
_F32 = jnp.float32
_BF16 = jnp.bfloat16


def _dot(a, b, dims=None):
    if dims is None:
        return jnp.dot(a, b, preferred_element_type=_F32)
    return lax.dot_general(a, b, dims, preferred_element_type=_F32)


def _rms(x, w):
    return x * lax.rsqrt(jnp.mean(x * x, axis=-1, keepdims=True) + EPS) * w


def _sigmoid(x):
    return 1.0 / (1.0 + jnp.exp(-x))


def _gelu_exact(x):
    return 0.5 * x * (1.0 + lax.erf(x * math.sqrt(0.5)))


def _same_chunk_causal(rows, chunk):
    shift = chunk.bit_length() - 1
    assert chunk == 1 << shift
    t = lax.broadcasted_iota(jnp.int32, (rows, rows), 0)
    s = lax.broadcasted_iota(jnp.int32, (rows, rows), 1)
    same = lax.shift_right_logical(t, shift) == lax.shift_right_logical(s, shift)
    return jnp.logical_and(same, s <= t)


def _layer_block(x, states_t, refs, *, group, chunk, mix_chunk, carry_state):
    (norm_w_ref, w_in_ref, lb_logits_ref, g_norm_w_ref, ln_v_w_ref, ln_v_b_ref,
     w_mix_ref, bias_mix_ref, w_out_ref, final_w_ref) = refs
    rows = x.shape[0]
    n_groups = rows // group
    chunks_per_group = group // chunk
    n_chunks = rows // chunk

    def lane_cat(a):
        if n_groups == 1:
            return a
        return jnp.concatenate([a[gi * group:(gi + 1) * group] for gi in range(n_groups)], axis=1)

    def lane_split(a):
        if n_groups == 1:
            return a
        w = a.shape[1] // n_groups
        return jnp.concatenate([a[:, gi * w:(gi + 1) * w] for gi in range(n_groups)], axis=0)

    h = _rms(x, norm_w_ref[...]).astype(_BF16)

    def proj(k):
        return _dot(h, w_in_ref[:, k * D_A:(k + 1) * D_A])

    logits = lb_logits_ref[...]
    e = jnp.exp(logits - jnp.max(logits, axis=0, keepdims=True))
    lb = e[0:1, :] / jnp.sum(e, axis=0, keepdims=True)

    q = proj(0)
    qs = q * _sigmoid(q) * (DK ** -0.5)
    f = lb + (1.0 - lb) * _sigmoid(proj(1))
    kk = 1.0 - f
    g = jnp.log(f)
    iv = proj(2).astype(_BF16)

    blk_mask = _same_chunk_causal(group, chunk)
    cum_mat = jnp.where(blk_mask, 1.0, 0.0).astype(_BF16)
    g_hi = g.astype(_BF16)
    g_lo = (g - g_hi.astype(_F32)).astype(_BF16)
    b = lane_split(_dot(cum_mat, lane_cat(g_hi)) + _dot(cum_mat, lane_cat(g_lo)))

    q_dec = (qs * jnp.exp(b)).astype(_BF16)
    k_inv = kk * jnp.exp(-b)
    k_inv_bf = k_inv.astype(_BF16)

    new_states = [[None] * H_A for _ in range(n_chunks)]
    o_heads = []
    for hd in range(H_A):
        ln = slice(hd * DK, (hd + 1) * DK)
        o_chunks = []
        s_t = states_t[0][hd]
        for gi in range(n_groups):
            rg = slice(gi * group, (gi + 1) * group)
            sc = _dot(q_dec[rg, ln], k_inv_bf[rg, ln], _NT)
            sc = jnp.where(blk_mask, sc, 0.0).astype(_BF16)
            intra = _dot(sc, iv[rg, ln])
            for ci in range(chunks_per_group):
                c = gi * chunks_per_group + ci
                rw = slice(c * chunk, (c + 1) * chunk)
                if not carry_state:
                    s_t = states_t[c][hd]
                inter = _dot(q_dec[rw, ln], s_t.astype(_BF16), _NT)
                o_chunks.append(inter + intra[ci * chunk:(ci + 1) * chunk])
                last = (c + 1) * chunk - 1
                dec_last = jnp.exp(b[last:last + 1, ln])
                k_upd = (k_inv[rw, ln] * dec_last).astype(_BF16)
                s_t = s_t * dec_last + _dot(iv[rw, ln], k_upd, _TN)
                new_states[c][hd] = s_t
        o_heads.append(jnp.concatenate(o_chunks, axis=0) if n_chunks > 1 else o_chunks[0])

    ga = proj(3)
    y_a_heads = []
    for hd in range(H_A):
        o = o_heads[hd]
        o = o * lax.rsqrt(jnp.mean(o * o, axis=-1, keepdims=True) + EPS)
        y_a_heads.append(o)
    y_a = jnp.concatenate(y_a_heads, axis=1) * g_norm_w_ref[...] * (ga * _sigmoid(ga))

    u = _gelu_exact(proj(4))
    v = _gelu_exact(proj(5))
    mu = jnp.mean(v, axis=-1, keepdims=True)
    vc = v - mu
    var = jnp.mean(vc * vc, axis=-1, keepdims=True)
    v_norm = vc * lax.rsqrt(var + EPS) * ln_v_w_ref[...] + ln_v_b_ref[...]
    gb = proj(6)
    v_bf = v_norm.astype(_BF16)
    mix_mask = blk_mask if mix_chunk == chunk else _same_chunk_causal(group, mix_chunk)
    mix_heads = []
    for hb in range(H_B):
        ln = slice(hb * C_B, (hb + 1) * C_B)
        w = jnp.where(mix_mask, w_mix_ref[hb], 0.0).astype(_BF16)
        mix_heads.append(lane_split(_dot(w, lane_cat(v_bf[:, ln]))))
    bias = bias_mix_ref[...]
    if n_groups > 1:
        bias = jnp.concatenate([bias] * n_groups, axis=0)
    mix = jnp.concatenate(mix_heads, axis=1) + bias
    y_b = u * mix * (gb * _sigmoid(gb))

    y = jnp.concatenate([y_a, y_b], axis=1).astype(_BF16)
    out = x + _dot(y, w_out_ref[...])
    return _rms(out, final_w_ref[...]), new_states, v_norm


def _prompt_kernel(x_ref, norm_w_ref, w_in_ref, lb_logits_ref, g_norm_w_ref, ln_v_w_ref,
                   ln_v_b_ref, w_mix_ref, bias_mix_ref, w_out_ref, final_w_ref,
                   y_ref, state_ref, state_t_scr):
    j = pl.program_id(1)

    @pl.when(j == 0)
    def _():
        state_t_scr[...] = jnp.zeros_like(state_t_scr)

    refs = (norm_w_ref, w_in_ref, lb_logits_ref, g_norm_w_ref, ln_v_w_ref, ln_v_b_ref,
            w_mix_ref, bias_mix_ref, w_out_ref, final_w_ref)
    states_t = [[state_t_scr[hd] for hd in range(H_A)]]
    y, new_states, _ = _layer_block(x_ref[0], states_t, refs, group=PROMPT_GROUP,
                                    chunk=PROMPT_CHUNK, mix_chunk=MLP_CHUNK, carry_state=True)
    y_ref[0] = y
    for hd in range(H_A):
        state_t_scr[hd] = new_states[-1][hd]

    @pl.when(j == pl.num_programs(1) - 1)
    def _():
        for hd in range(H_A):
            state_ref[0, hd] = state_t_scr[hd].T


def _sample_kernel(x_ref, state_in_ref, norm_w_ref, w_in_ref, lb_logits_ref, g_norm_w_ref,
                   ln_v_w_ref, ln_v_b_ref, w_mix_ref, bias_mix_ref, w_out_ref, final_w_ref,
                   y_ref, state_out_ref, v_ref, *, n_streams, frames):
    refs = (norm_w_ref, w_in_ref, lb_logits_ref, g_norm_w_ref, ln_v_w_ref, ln_v_b_ref,
            w_mix_ref, bias_mix_ref, w_out_ref, final_w_ref)
    states_t = [[state_in_ref[s, hd].T for hd in range(H_A)] for s in range(n_streams)]
    y, new_states, v_norm = _layer_block(x_ref[...], states_t, refs, group=n_streams * frames,
                                         chunk=frames, mix_chunk=frames, carry_state=False)
    y_ref[...] = y
    v_ref[...] = v_norm
    for s in range(n_streams):
        for hd in range(H_A):
            state_out_ref[s, hd] = new_states[s][hd].T


def _full(shape):
    return pl.BlockSpec(shape, lambda *_: (0,) * len(shape))


def _mix_operands(w_s, b_s, rows, mix_chunk):
    reps = rows // mix_chunk
    w_rep = jnp.tile(w_s[:, :mix_chunk, :mix_chunk], (1, reps, reps))
    bias = jnp.tile(b_s[:, :mix_chunk].T, (reps, 1))
    bias = jnp.repeat(bias, C_B, axis=1)
    return w_rep, bias


def kernel(x_prompt, x_sample, state_hgrn, norm_w, w_in, lb_logits, g_norm_w, ln_v_w, ln_v_b, w_s, b_s, w_out, final_norm_w):
    batch, seq, _ = x_prompt.shape
    dec_batch, dec_seq, _ = x_sample.shape
    assert norm_w.shape[0] == 1, "single-layer trunk"
    assert seq % PROMPT_BLOCK == 0 and PROMPT_BLOCK % PROMPT_GROUP == 0
    assert PROMPT_GROUP % MLP_CHUNK == 0 and MLP_CHUNK % PROMPT_CHUNK == 0

    w_in_bf = w_in[0].astype(_BF16)
    w_out_bf = w_out[0].astype(_BF16)
    row = lambda a: a.reshape(1, -1)
    params = (row(norm_w[0]), w_in_bf, lb_logits, row(g_norm_w[0]), row(ln_v_w[0]), row(ln_v_b[0]))
    tail = (w_out_bf, row(final_norm_w))
    param_specs = [_full((1, D_MODEL)), _full((D_MODEL, IN_WIDTH)), _full(lb_logits.shape),
                   _full((1, D_A)), _full((1, D_B)), _full((1, D_B))]
    tail_specs = [_full((D_MODEL, D_MODEL)), _full((1, D_MODEL))]

    rp = PROMPT_BLOCK
    w_rep, bias = _mix_operands(w_s[0], b_s[0], PROMPT_GROUP, MLP_CHUNK)
    y_prompt, state_p = pl.pallas_call(
        _prompt_kernel,
        grid=(batch, seq // rp),
        in_specs=[pl.BlockSpec((1, rp, D_MODEL), lambda b, j: (b, j, 0))] + param_specs
                 + [_full((H_B, PROMPT_GROUP, PROMPT_GROUP)), _full((PROMPT_GROUP, D_B))] + tail_specs,
        out_specs=[pl.BlockSpec((1, rp, D_MODEL), lambda b, j: (b, j, 0)),
                   pl.BlockSpec((1, H_A, DK, DV), lambda b, j: (b, 0, 0, 0))],
        out_shape=[jax.ShapeDtypeStruct((batch, seq, D_MODEL), _F32),
                   jax.ShapeDtypeStruct((batch, H_A, DK, DV), _F32)],
        scratch_shapes=[pltpu.VMEM((H_A, DV, DK), _F32)],
        compiler_params=pltpu.CompilerParams(
            dimension_semantics=("arbitrary", "arbitrary"), vmem_limit_bytes=VMEM_LIMIT_BYTES),
        name="prompt_layer",
    )(x_prompt, *params, w_rep, bias, *tail)

    rs = dec_batch * dec_seq
    mix_len = min(MLP_CHUNK, dec_seq)
    assert dec_seq <= HGRN_CHUNK and dec_seq == mix_len and dec_seq & (dec_seq - 1) == 0
    w_rep_s, bias_s = _mix_operands(w_s[0], b_s[0], rs, mix_len)
    y_sample, state_s, v_sample = pl.pallas_call(
        functools.partial(_sample_kernel, n_streams=dec_batch, frames=dec_seq),
        grid=(1,),
        in_specs=[_full((rs, D_MODEL)), _full((dec_batch, H_A, DK, DV))] + param_specs
                 + [_full((H_B, rs, rs)), _full((rs, D_B))] + tail_specs,
        out_specs=[_full((rs, D_MODEL)), _full((dec_batch, H_A, DK, DV)), _full((rs, D_B))],
        out_shape=[jax.ShapeDtypeStruct((rs, D_MODEL), _F32),
                   jax.ShapeDtypeStruct((dec_batch, H_A, DK, DV), _F32),
                   jax.ShapeDtypeStruct((rs, D_B), _F32)],
        compiler_params=pltpu.CompilerParams(
            dimension_semantics=("arbitrary",), vmem_limit_bytes=VMEM_LIMIT_BYTES),
        name="sample_layer",
    )(x_sample.reshape(rs, D_MODEL), state_hgrn[0], *params, w_rep_s, bias_s, *tail)

    return (y_prompt,
            y_sample.reshape(dec_batch, dec_seq, D_MODEL),
            state_p[None],
            state_s[None],
            v_sample.reshape(1, dec_batch, dec_seq, D_B))
```

```python
import functools
import math

import jax
import jax.numpy as jnp
from jax import lax
from jax.experimental import pallas as pl
from jax.experimental.pallas import tpu as pltpu

D_MODEL = 1024
D_A = 512
D_B = 512
H_A = 4
DK = 128
DV = 128
H_B = 4
C_B = 128
MLP_CHUNK = 128
HGRN_CHUNK = 64
IN_WIDTH = 4 * D_A + 3 * D_B
EPS = 1e-6

PROMPT_BLOCK = 512
PROMPT_GROUP = 128
PROMPT_CHUNK = 128
VMEM_LIMIT_BYTES = 56 * 1024 * 1024

_NT = (((1,), (1,)), ((), ()))
_TN = (((0,), (0,)), ((), ()))
_F32 = jnp.float32
_BF16 = jnp.bfloat16


def _dot(a, b, dims=None):
    if dims is None:
        return jnp.dot(a, b, preferred_element_type=_F32)
    return lax.dot_general(a, b, dims, preferred_element_type=_F32)


def _rms(x, w):
    return x * lax.rsqrt(jnp.mean(x * x, axis=-1, keepdims=True) + EPS) * w


def _sigmoid(x):
    return 1.0 / (1.0 + jnp.exp(-x))


def _gelu_exact(x):
    return 0.5 * x * (1.0 + lax.erf(x * math.sqrt(0.5)))


def _same_chunk_causal(rows, chunk):
    shift = chunk.bit_length() - 1
    assert chunk == 1 << shift
    t = lax.broadcasted_iota(jnp.int32, (rows, rows), 0)
    s = lax.broadcasted_iota(jnp.int32, (rows, rows), 1)
    same = lax.shift_right_logical(t, shift) == lax.shift_right_logical(s, shift)
    return jnp.logical_and(same, s <= t)


def _layer_block(x, states_t, refs, *, group, chunk, mix_chunk, carry_state):
    (norm_w_ref, w_in_ref, lb_logits_ref, g_norm_w_ref, ln_v_w_ref, ln_v_b_ref,
     w_s_ref, b_s_ref, w_out_ref, final_w_ref) = refs
    rows = x.shape[0]
    n_groups = rows // group
    chunks_per_group = group // chunk
    n_chunks = rows // chunk

    def lane_cat(a):
        if n_groups == 1:
            return a
        return jnp.concatenate([a[gi * group:(gi + 1) * group] for gi in range(n_groups)], axis=1)

    def lane_split(a):
        if n_groups == 1:
            return a
        w = a.shape[1] // n_groups
        return jnp.concatenate([a[:, gi * w:(gi + 1) * w] for gi in range(n_groups)], axis=0)

    h = _rms(x, norm_w_ref[...]).astype(_BF16)

    def proj(k):
        return _dot(h, w_in_ref[:, k * D_A:(k + 1) * D_A])

    logits = lb_logits_ref[...]
    e = jnp.exp(logits - jnp.max(logits, axis=0, keepdims=True))
    lb = e[0:1, :] / jnp.sum(e, axis=0, keepdims=True)

    q = proj(0)
    qs = q * _sigmoid(q) * (DK ** -0.5)
    f = lb + (1.0 - lb) * _sigmoid(proj(1))
    kk = 1.0 - f
    g = jnp.log(f)
    iv = proj(2).astype(_BF16)

    blk_mask = _same_chunk_causal(group, chunk)
    cum_mat = jnp.where(blk_mask, 1.0, 0.0).astype(_BF16)
    g_hi = g.astype(_BF16)
    g_lo = (g - g_hi.astype(_F32)).astype(_BF16)
    b = lane_split(_dot(cum_mat, lane_cat(g_hi)) + _dot(cum_mat, lane_cat(g_lo)))

    q_dec = (qs * jnp.exp(b)).astype(_BF16)
    k_inv = kk * jnp.exp(-b)
    k_inv_bf = k_inv.astype(_BF16)

    new_states = [[None] * H_A for _ in range(n_chunks)]
    o_heads = []
    for hd in range(H_A):
        ln = slice(hd * DK, (hd + 1) * DK)
        o_chunks = []
        s_t = states_t[0][hd]
        for gi in range(n_groups):
            rg = slice(gi * group, (gi + 1) * group)
            sc = _dot(q_dec[rg, ln], k_inv_bf[rg, ln], _NT)
            sc = jnp.where(blk_mask, sc, 0.0).astype(_BF16)
            intra = _dot(sc, iv[rg, ln])
            for ci in range(chunks_per_group):
                c = gi * chunks_per_group + ci
                rw = slice(c * chunk, (c + 1) * chunk)
                if not carry_state:
                    s_t = states_t[c][hd]
                inter = _dot(q_dec[rw, ln], s_t.astype(_BF16), _NT)
                o_chunks.append(inter + intra[ci * chunk:(ci + 1) * chunk])
                last = (c + 1) * chunk - 1
                dec_last = jnp.exp(b[last:last + 1, ln])
                k_upd = (k_inv[rw, ln] * dec_last).astype(_BF16)
                s_t = s_t * dec_last + _dot(iv[rw, ln], k_upd, _TN)
                new_states[c][hd] = s_t
        o_heads.append(jnp.concatenate(o_chunks, axis=0) if n_chunks > 1 else o_chunks[0])

    ga = proj(3)
    y_a_heads = []
    for hd in range(H_A):
        o = o_heads[hd]
        o = o * lax.rsqrt(jnp.mean(o * o, axis=-1, keepdims=True) + EPS)
        y_a_heads.append(o)
    y_a = jnp.concatenate(y_a_heads, axis=1) * g_norm_w_ref[...] * (ga * _sigmoid(ga))

    u = _gelu_exact(proj(4))
    v = _gelu_exact(proj(5))
    mu = jnp.mean(v, axis=-1, keepdims=True)
    vc = v - mu
    var = jnp.mean(vc * vc, axis=-1, keepdims=True)
    v_norm = vc * lax.rsqrt(var + EPS) * ln_v_w_ref[...] + ln_v_b_ref[...]
    gb = proj(6)
    v_bf = v_norm.astype(_BF16)
    mix_mask = blk_mask if mix_chunk == chunk else _same_chunk_causal(group, mix_chunk)
    pos_r = lax.broadcasted_iota(jnp.int32, (group, MLP_CHUNK), 0)
    pos_c = lax.broadcasted_iota(jnp.int32, (group, MLP_CHUNK), 1)
    pos_mask = jnp.bitwise_and(pos_r, mix_chunk - 1) == pos_c
    mix_heads = []
    for hb in range(H_B):
        ln = slice(hb * C_B, (hb + 1) * C_B)
        if group == mix_chunk == MLP_CHUNK:
            w_full = w_s_ref[hb]
        else:
            w_rows = jnp.concatenate([w_s_ref[hb][:mix_chunk]] * (group // mix_chunk), axis=0)
            w_full = _dot(w_rows.astype(_BF16), jnp.where(pos_mask, 1.0, 0.0).astype(_BF16), _NT)
        w = jnp.where(mix_mask, w_full, 0.0).astype(_BF16)
        bias = jnp.sum(jnp.where(pos_mask, b_s_ref[hb:hb + 1, :], 0.0), axis=1, keepdims=True)
        mixed = _dot(w, lane_cat(v_bf[:, ln])) + bias
        mix_heads.append(lane_split(mixed))
    mix = jnp.concatenate(mix_heads, axis=1)
    y_b = u * mix * (gb * _sigmoid(gb))

    y = jnp.concatenate([y_a, y_b], axis=1).astype(_BF16)
    out = x + _dot(y, w_out_ref[...])
    return _rms(out, final_w_ref[...]), new_states, v_norm


def _fused_kernel(xp_ref, xs_ref, state_in_ref, norm_w_ref, w_in_ref, lb_logits_ref, g_norm_w_ref,
                  ln_v_w_ref, ln_v_b_ref, w_s_ref, b_s_ref, w_out_ref, final_w_ref,
                  yp_ref, state_p_ref, ys_ref, state_s_ref, v_ref, state_t_scr,
                  *, prompt_steps, steps_per_stream, n_streams, frames):
    step = pl.program_id(0)
    refs = (norm_w_ref, w_in_ref, lb_logits_ref, g_norm_w_ref, ln_v_w_ref, ln_v_b_ref,
            w_s_ref, b_s_ref, w_out_ref, final_w_ref)

    @pl.when(step < prompt_steps)
    def _prompt():
        j = lax.rem(step, steps_per_stream)

        @pl.when(j == 0)
        def _():
            state_t_scr[...] = jnp.zeros_like(state_t_scr)

        states_t = [[state_t_scr[hd] for hd in range(H_A)]]
        y, new_states, _ = _layer_block(xp_ref[0], states_t, refs, group=PROMPT_GROUP,
                                        chunk=PROMPT_CHUNK, mix_chunk=MLP_CHUNK, carry_state=True)
        yp_ref[0] = y
        for hd in range(H_A):
            state_t_scr[hd] = new_states[-1][hd]

        @pl.when(j == steps_per_stream - 1)
        def _():
            for hd in range(H_A):
                state_p_ref[0, hd] = state_t_scr[hd].T

    @pl.when(step == prompt_steps)
    def _sample():
        states_t = [[state_in_ref[s, hd].T for hd in range(H_A)] for s in range(n_streams)]
        y, new_states, v_norm = _layer_block(xs_ref[...], states_t, refs, group=n_streams * frames,
                                             chunk=frames, mix_chunk=frames, carry_state=False)
        ys_ref[...] = y
        v_ref[...] = v_norm
        for s in range(n_streams):
            for hd in range(H_A):
                state_s_ref[s, hd] = new_states[s][hd].T


def _full(shape):
    return pl.BlockSpec(shape, lambda *_: (0,) * len(shape))


def kernel(x_prompt, x_sample, state_hgrn, norm_w, w_in, lb_logits, g_norm_w, ln_v_w, ln_v_b, w_s, b_s, w_out, final_norm_w):
    batch, seq, _ = x_prompt.shape
    dec_batch, dec_seq, _ = x_sample.shape
    assert norm_w.shape[0] == 1, "single-layer trunk"
    assert seq % PROMPT_BLOCK == 0 and PROMPT_BLOCK % PROMPT_GROUP == 0
    assert PROMPT_GROUP % MLP_CHUNK == 0 and MLP_CHUNK % PROMPT_CHUNK == 0
    assert dec_seq <= HGRN_CHUNK and dec_seq <= MLP_CHUNK and dec_seq & (dec_seq - 1) == 0
    assert w_s.shape[2:] == (MLP_CHUNK, MLP_CHUNK)

    rp = PROMPT_BLOCK
    rs = dec_batch * dec_seq
    steps_per_stream = seq // rp
    prompt_steps = batch * steps_per_stream
    row = lambda a: a.reshape(1, -1)

    def prompt_block(step):
        blk = jnp.minimum(step, prompt_steps - 1)
        return blk // steps_per_stream, blk % steps_per_stream

    def x_map(step):
        b, j = prompt_block(step)
        return b, j, 0

    def state_map(step):
        b, _ = prompt_block(step)
        return b, 0, 0, 0

    y_prompt, state_p, y_sample, state_s, v_sample = pl.pallas_call(
        functools.partial(_fused_kernel, prompt_steps=prompt_steps, steps_per_stream=steps_per_stream,
                          n_streams=dec_batch, frames=dec_seq),
        grid=(prompt_steps + 1,),
        in_specs=[pl.BlockSpec((1, rp, D_MODEL), x_map),
                  _full((rs, D_MODEL)), _full((dec_batch, H_A, DK, DV)),
                  _full((1, D_MODEL)), _full((D_MODEL, IN_WIDTH)), _full(lb_logits.shape),
                  _full((1, D_A)), _full((1, D_B)), _full((1, D_B)),
                  _full((H_B, MLP_CHUNK, MLP_CHUNK)), _full((H_B, MLP_CHUNK)),
                  _full((D_MODEL, D_MODEL)), _full((1, D_MODEL))],
        out_specs=[pl.BlockSpec((1, rp, D_MODEL), x_map),
                   pl.BlockSpec((1, H_A, DK, DV), state_map),
                   _full((rs, D_MODEL)), _full((dec_batch, H_A, DK, DV)), _full((rs, D_B))],
        out_shape=[jax.ShapeDtypeStruct((batch, seq, D_MODEL), _F32),
                   jax.ShapeDtypeStruct((batch, H_A, DK, DV), _F32),
                   jax.ShapeDtypeStruct((rs, D_MODEL), _F32),
                   jax.ShapeDtypeStruct((dec_batch, H_A, DK, DV), _F32),
                   jax.ShapeDtypeStruct((rs, D_B), _F32)],
        scratch_shapes=[pltpu.VMEM((H_A, DV, DK), _F32)],
        compiler_params=pltpu.CompilerParams(
            dimension_semantics=("arbitrary",), vmem_limit_bytes=VMEM_LIMIT_BYTES),
        name="layer_step",
    )(x_prompt, x_sample.reshape(rs, D_MODEL), state_hgrn[0],
      row(norm_w[0]), w_in[0].astype(_BF16), lb_logits, row(g_norm_w[0]), row(ln_v_w[0]), row(ln_v_b[0]),
      w_s[0], b_s[0], w_out[0].astype(_BF16), row(final_norm_w))

    return (y_prompt,
            y_sample.reshape(dec_batch, dec_seq, D_MODEL),
            state_p[None],
            state_s[None],
            v_sample.reshape(1, dec_batch, dec_seq, D_B))
```

```python
import functools
import math

import jax
import jax.numpy as jnp
from jax import lax
from jax.experimental import pallas as pl
from jax.experimental.pallas import tpu as pltpu

D_MODEL = 1024
D_A = 512
D_B = 512
H_A = 4
DK = 128
DV = 128
H_B = 4
C_B = 128
MLP_CHUNK = 128
HGRN_CHUNK = 64
IN_WIDTH = 4 * D_A + 3 * D_B
EPS = 1e-6

PROMPT_BLOCK = 512
PROMPT_GROUP = 128
PROMPT_CHUNK = 128
SUBLANES = 8
LOG_DECAY_LIMIT = 78.0
VMEM_LIMIT_BYTES = 56 * 1024 * 1024

_NT = (((1,), (1,)), ((), ()))
_TN = (((0,), (0,)), ((), ()))
_F32 = jnp.float32
_BF16 = jnp.bfloat16


def _dot(a, b, dims=None):
    if dims is None:
        return jnp.dot(a, b, preferred_element_type=_F32)
    return lax.dot_general(a, b, dims, preferred_element_type=_F32)


def _rms(x, w):
    return x * lax.rsqrt(jnp.mean(x * x, axis=-1, keepdims=True) + EPS) * w


def _sigmoid(x):
    return 1.0 / (1.0 + jnp.exp(-x))


def _gelu_exact(x):
    return 0.5 * x * (1.0 + lax.erf(x * math.sqrt(0.5)))


def _same_chunk_causal(rows, chunk):
    shift = chunk.bit_length() - 1
    assert chunk == 1 << shift
    t = lax.broadcasted_iota(jnp.int32, (rows, rows), 0)
    s = lax.broadcasted_iota(jnp.int32, (rows, rows), 1)
    same = lax.shift_right_logical(t, shift) == lax.shift_right_logical(s, shift)
    return jnp.logical_and(same, s <= t)


def _lane_cat(a, group):
    n = a.shape[0] // group
    if n == 1:
        return a
    return jnp.concatenate([a[gi * group:(gi + 1) * group] for gi in range(n)], axis=1)


def _lane_split(a, n):
    if n == 1:
        return a
    w = a.shape[1] // n
    return jnp.concatenate([a[:, gi * w:(gi + 1) * w] for gi in range(n)], axis=0)


def _chunk_cumsum(g, mask, group):
    n = g.shape[0] // group
    cum_mat = jnp.where(mask, 1.0, 0.0).astype(_BF16)
    g_hi = g.astype(_BF16)
    g_lo = (g - g_hi.astype(_F32)).astype(_BF16)
    return _lane_split(_dot(cum_mat, _lane_cat(g_hi, group)) + _dot(cum_mat, _lane_cat(g_lo, group)), n)


def _recurrence_fast(qs, kk, g, iv, *, states_t, group, chunk, carry_state):
    rows = qs.shape[0]
    n_groups = rows // group
    chunks_per_group = group // chunk
    n_chunks = rows // chunk
    blk_mask = _same_chunk_causal(group, chunk)
    b = _chunk_cumsum(g, blk_mask, group)

    iv = iv.astype(_BF16)
    q_dec = (qs * jnp.exp(b)).astype(_BF16)
    k_inv = kk * jnp.exp(-b)
    k_inv_bf = k_inv.astype(_BF16)

    low = None
    for c in range(n_chunks):
        b_last = b[(c + 1) * chunk - 1:(c + 1) * chunk, :]
        low = b_last if low is None else jnp.minimum(low, b_last)

    new_states = [[None] * H_A for _ in range(n_chunks)]
    o_heads = []
    for hd in range(H_A):
        ln = slice(hd * DK, (hd + 1) * DK)
        o_chunks = []
        s_t = states_t[0][hd]
        for gi in range(n_groups):
            rg = slice(gi * group, (gi + 1) * group)
            sc = _dot(q_dec[rg, ln], k_inv_bf[rg, ln], _NT)
            sc = jnp.where(blk_mask, sc, 0.0).astype(_BF16)
            intra = _dot(sc, iv[rg, ln])
            for ci in range(chunks_per_group):
                c = gi * chunks_per_group + ci
                rw = slice(c * chunk, (c + 1) * chunk)
                if not carry_state:
                    s_t = states_t[c][hd]
                inter = _dot(q_dec[rw, ln], s_t.astype(_BF16), _NT)
                o_chunks.append(inter + intra[ci * chunk:(ci + 1) * chunk])
                last = (c + 1) * chunk - 1
                dec_last = jnp.exp(b[last:last + 1, ln])
                k_upd = (k_inv[rw, ln] * dec_last).astype(_BF16)
                s_t = s_t * dec_last + _dot(iv[rw, ln], k_upd, _TN)
                new_states[c][hd] = s_t
        o_heads.append(jnp.concatenate(o_chunks, axis=0) if n_chunks > 1 else o_chunks[0])
    return jnp.concatenate(o_heads, axis=1), new_states, jnp.min(low)


def _recurrence_exact(qs, kk, g, iv, *, work_ref, state_ref, group, chunk):
    rows = qs.shape[0]
    assert chunk % SUBLANES == 0 and group % SUBLANES == 0
    minis_per_chunk = chunk // SUBLANES
    b = _chunk_cumsum(g, _same_chunk_causal(group, SUBLANES), group)
    for k, a in enumerate((qs, kk, b, iv)):
        work_ref[k, 0:rows, :] = a
    t_idx = lax.broadcasted_iota(jnp.int32, (SUBLANES, DK), 0)

    def mini(m, carry):
        r0 = pl.multiple_of(m * SUBLANES, SUBLANES)
        qs8, kk8, b8, iv8 = (work_ref[k, pl.ds(r0, SUBLANES), :] for k in range(4))
        c = m // minis_per_chunk
        for hd in range(H_A):
            ln = slice(hd * DK, (hd + 1) * DK)
            q_h, k_h, b_h, v_h = qs8[:, ln], kk8[:, ln], b8[:, ln], iv8[:, ln]
            s_t = state_ref[c, hd]
            acc = _dot(q_h * jnp.exp(b_h), s_t, _NT)
            for s in range(SUBLANES):
                dec = jnp.exp(jnp.minimum(b_h - b_h[s:s + 1, :], 0.0))
                term = jnp.where(t_idx >= s, q_h * k_h[s:s + 1, :] * dec, 0.0)
                acc = acc + jnp.sum(term, axis=1, keepdims=True) * v_h[s:s + 1, :]
            work_ref[4, pl.ds(r0, SUBLANES), ln] = acc
            b_end = b_h[SUBLANES - 1:SUBLANES, :]
            k_upd = k_h * jnp.exp(b_end - b_h)
            state_ref[c, hd] = s_t * jnp.exp(b_end) + _dot(v_h, k_upd, _TN)
        return carry

    lax.fori_loop(0, rows // SUBLANES, mini, 0)
    return work_ref[4, 0:rows, :]


def _layer_block(x, refs, recurrence, *, group, mix_chunk):
    (norm_w_ref, w_in_ref, lb_logits_ref, g_norm_w_ref, ln_v_w_ref, ln_v_b_ref,
     w_s_ref, b_s_ref, w_out_ref, final_w_ref) = refs
    n_groups = x.shape[0] // group

    h = _rms(x, norm_w_ref[...]).astype(_BF16)

    def proj(k):
        return _dot(h, w_in_ref[:, k * D_A:(k + 1) * D_A])

    logits = lb_logits_ref[...]
    e = jnp.exp(logits - jnp.max(logits, axis=0, keepdims=True))
    lb = e[0:1, :] / jnp.sum(e, axis=0, keepdims=True)

    q = proj(0)
    qs = q * _sigmoid(q) * (DK ** -0.5)
    f = lb + (1.0 - lb) * _sigmoid(proj(1))
    o, aux = recurrence(qs, 1.0 - f, jnp.log(f), proj(2))

    ga = proj(3)
    y_a_heads = []
    for hd in range(H_A):
        o_h = o[:, hd * DV:(hd + 1) * DV]
        y_a_heads.append(o_h * lax.rsqrt(jnp.mean(o_h * o_h, axis=-1, keepdims=True) + EPS))
    y_a = jnp.concatenate(y_a_heads, axis=1) * g_norm_w_ref[...] * (ga * _sigmoid(ga))

    u = _gelu_exact(proj(4))
    v = _gelu_exact(proj(5))
    mu = jnp.mean(v, axis=-1, keepdims=True)
    vc = v - mu
    var = jnp.mean(vc * vc, axis=-1, keepdims=True)
    v_norm = vc * lax.rsqrt(var + EPS) * ln_v_w_ref[...] + ln_v_b_ref[...]
    gb = proj(6)
    v_bf = v_norm.astype(_BF16)
    mix_mask = _same_chunk_causal(group, mix_chunk)
    pos_r = lax.broadcasted_iota(jnp.int32, (group, MLP_CHUNK), 0)
    pos_c = lax.broadcasted_iota(jnp.int32, (group, MLP_CHUNK), 1)
    pos_mask = jnp.bitwise_and(pos_r, mix_chunk - 1) == pos_c
    mix_heads = []
    for hb in range(H_B):
        ln = slice(hb * C_B, (hb + 1) * C_B)
        if group == mix_chunk == MLP_CHUNK:
            w_full = w_s_ref[hb]
        else:
            w_rows = jnp.concatenate([w_s_ref[hb][:mix_chunk]] * (group // mix_chunk), axis=0)
            w_full = _dot(w_rows.astype(_BF16), jnp.where(pos_mask, 1.0, 0.0).astype(_BF16), _NT)
        w = jnp.where(mix_mask, w_full, 0.0).astype(_BF16)
        bias = jnp.sum(jnp.where(pos_mask, b_s_ref[hb:hb + 1, :], 0.0), axis=1, keepdims=True)
        mixed = _dot(w, _lane_cat(v_bf[:, ln], group)) + bias
        mix_heads.append(_lane_split(mixed, n_groups))
    mix = jnp.concatenate(mix_heads, axis=1)
    y_b = u * mix * (gb * _sigmoid(gb))

    y = jnp.concatenate([y_a, y_b], axis=1).astype(_BF16)
    out = x + _dot(y, w_out_ref[...])
    return _rms(out, final_w_ref[...]), v_norm, aux


def _fused_kernel(xp_ref, xs_ref, state_in_ref, norm_w_ref, w_in_ref, lb_logits_ref, g_norm_w_ref,
                  ln_v_w_ref, ln_v_b_ref, w_s_ref, b_s_ref, w_out_ref, final_w_ref,
                  yp_ref, state_p_ref, ys_ref, state_s_ref, v_ref,
                  state_t_scr, state_prev_scr, work_scr,
                  *, prompt_steps, steps_per_stream, n_streams, frames):
    step = pl.program_id(0)
    refs = (norm_w_ref, w_in_ref, lb_logits_ref, g_norm_w_ref, ln_v_w_ref, ln_v_b_ref,
            w_s_ref, b_s_ref, w_out_ref, final_w_ref)

    def too_steep(low):
        return jnp.logical_not(low >= -LOG_DECAY_LIMIT)

    @pl.when(step < prompt_steps)
    def _prompt():
        j = lax.rem(step, steps_per_stream)

        @pl.when(j == 0)
        def _():
            state_t_scr[...] = jnp.zeros_like(state_t_scr)

        state_prev_scr[...] = state_t_scr[...]
        fast = functools.partial(
            _recurrence_fast, states_t=[[state_t_scr[0, hd] for hd in range(H_A)]],
            group=PROMPT_GROUP, chunk=PROMPT_CHUNK, carry_state=True)
        y, _, (new_states, low) = _layer_block(
            xp_ref[0], refs, lambda *a: _split_aux(fast(*a)), group=PROMPT_GROUP, mix_chunk=MLP_CHUNK)
        yp_ref[0] = y
        for hd in range(H_A):
            state_t_scr[0, hd] = new_states[-1][hd]

        @pl.when(too_steep(low))
        def _():
            state_t_scr[...] = state_prev_scr[...]
            exact = functools.partial(_recurrence_exact, work_ref=work_scr, state_ref=state_t_scr,
                                      group=PROMPT_GROUP, chunk=PROMPT_BLOCK)
            y_exact, _, _ = _layer_block(xp_ref[0], refs, lambda *a: (exact(*a), None),
                                         group=PROMPT_GROUP, mix_chunk=MLP_CHUNK)
            yp_ref[0] = y_exact

        @pl.when(j == steps_per_stream - 1)
        def _():
            for hd in range(H_A):
                state_p_ref[0, hd] = state_t_scr[0, hd].T

    @pl.when(step == prompt_steps)
    def _sample():
        rows = n_streams * frames
        fast = functools.partial(
            _recurrence_fast,
            states_t=[[state_in_ref[s, hd].T for hd in range(H_A)] for s in range(n_streams)],
            group=rows, chunk=frames, carry_state=False)
        y, v_norm, (new_states, low) = _layer_block(
            xs_ref[...], refs, lambda *a: _split_aux(fast(*a)), group=rows, mix_chunk=frames)
        ys_ref[...] = y
        v_ref[...] = v_norm
        for s in range(n_streams):
            for hd in range(H_A):
                state_s_ref[s, hd] = new_states[s][hd].T

        @pl.when(too_steep(low))
        def _():
            for s in range(n_streams):
                for hd in range(H_A):
                    state_s_ref[s, hd] = state_in_ref[s, hd].T
            exact = functools.partial(_recurrence_exact, work_ref=work_scr, state_ref=state_s_ref,
                                      group=rows, chunk=frames)
            y_exact, v_exact, _ = _layer_block(xs_ref[...], refs, lambda *a: (exact(*a), None),
                                               group=rows, mix_chunk=frames)
            ys_ref[...] = y_exact
            v_ref[...] = v_exact
            for s in range(n_streams):
                for hd in range(H_A):
                    state_s_ref[s, hd] = state_s_ref[s, hd].T


def _split_aux(res):
    o, new_states, low = res
    return o, (new_states, low)


def _full(shape):
    return pl.BlockSpec(shape, lambda *_: (0,) * len(shape))


def kernel(x_prompt, x_sample, state_hgrn, norm_w, w_in, lb_logits, g_norm_w, ln_v_w, ln_v_b, w_s, b_s, w_out, final_norm_w):
    batch, seq, _ = x_prompt.shape
    dec_batch, dec_seq, _ = x_sample.shape
    assert norm_w.shape[0] == 1, "single-layer trunk"
    assert seq % PROMPT_BLOCK == 0 and PROMPT_BLOCK % PROMPT_GROUP == 0
    assert PROMPT_GROUP % MLP_CHUNK == 0 and MLP_CHUNK % PROMPT_CHUNK == 0
    assert dec_seq <= HGRN_CHUNK and dec_seq <= MLP_CHUNK and dec_seq & (dec_seq - 1) == 0
    assert w_s.shape[2:] == (MLP_CHUNK, MLP_CHUNK)

    rp = PROMPT_BLOCK
    rs = dec_batch * dec_seq
    steps_per_stream = seq // rp
    prompt_steps = batch * steps_per_stream
    row = lambda a: a.reshape(1, -1)

    def prompt_block(step):
        blk = jnp.minimum(step, prompt_steps - 1)
        return blk // steps_per_stream, blk % steps_per_stream

    def x_map(step):
        b, j = prompt_block(step)
        return b, j, 0

    def state_map(step):
        b, _ = prompt_block(step)
        return b, 0, 0, 0

    y_prompt, state_p, y_sample, state_s, v_sample = pl.pallas_call(
        functools.partial(_fused_kernel, prompt_steps=prompt_steps, steps_per_stream=steps_per_stream,
                          n_streams=dec_batch, frames=dec_seq),
        grid=(prompt_steps + 1,),
        in_specs=[pl.BlockSpec((1, rp, D_MODEL), x_map),
                  _full((rs, D_MODEL)), _full((dec_batch, H_A, DK, DV)),
                  _full((1, D_MODEL)), _full((D_MODEL, IN_WIDTH)), _full(lb_logits.shape),
                  _full((1, D_A)), _full((1, D_B)), _full((1, D_B)),
                  _full((H_B, MLP_CHUNK, MLP_CHUNK)), _full((H_B, MLP_CHUNK)),
                  _full((D_MODEL, D_MODEL)), _full((1, D_MODEL))],
        out_specs=[pl.BlockSpec((1, rp, D_MODEL), x_map),
                   pl.BlockSpec((1, H_A, DK, DV), state_map),
                   _full((rs, D_MODEL)), _full((dec_batch, H_A, DK, DV)), _full((rs, D_B))],
        out_shape=[jax.ShapeDtypeStruct((batch, seq, D_MODEL), _F32),
                   jax.ShapeDtypeStruct((batch, H_A, DK, DV), _F32),
                   jax.ShapeDtypeStruct((rs, D_MODEL), _F32),
                   jax.ShapeDtypeStruct((dec_batch, H_A, DK, DV), _F32),
                   jax.ShapeDtypeStruct((rs, D_B), _F32)],
        scratch_shapes=[pltpu.VMEM((1, H_A, DV, DK), _F32),
                        pltpu.VMEM((1, H_A, DV, DK), _F32),
                        pltpu.VMEM((5, max(rp, rs), D_A), _F32)],
        compiler_params=pltpu.CompilerParams(
            dimension_semantics=("arbitrary",), vmem_limit_bytes=VMEM_LIMIT_BYTES),
        name="layer_step",
    )(x_prompt, x_sample.reshape(rs, D_MODEL), state_hgrn[0],
      row(norm_w[0]), w_in[0].astype(_BF16), lb_logits, row(g_norm_w[0]), row(ln_v_w[0]), row(ln_v_b[0]),
      w_s[0], b_s[0], w_out[0].astype(_BF16), row(final_norm_w))

    return (y_prompt,
            y_sample.reshape(dec_batch, dec_seq, D_MODEL),
            state_p[None],
            state_s[None],
            v_sample.reshape(1, dec_batch, dec_seq, D_B))
```

```python
import functools
import math

import jax
import jax.numpy as jnp
from jax import lax
from jax.experimental import pallas as pl
from jax.experimental.pallas import tpu as pltpu

D_MODEL = 1024
D_A = 512
D_B = 512
H_A = 4
DK = 128
DV = 128
H_B = 4
C_B = 128
MLP_CHUNK = 128
HGRN_CHUNK = 64
IN_WIDTH = 4 * D_A + 3 * D_B
EPS = 1e-6

PROMPT_BLOCK = 512
PROMPT_GROUP = 128
PROMPT_CHUNK = 128
SUBLANES = 8
LOG_DECAY_LIMIT = 78.0
VMEM_LIMIT_BYTES = 56 * 1024 * 1024
PROJECTION_PLAN = {0: (1, 0), 1: (2, 4), 2: (5,), 3: (3, 6)}

_NT = (((1,), (1,)), ((), ()))
_TN = (((0,), (0,)), ((), ()))
_F32 = jnp.float32
_BF16 = jnp.bfloat16


def _dot(a, b, dims=None):
    if dims is None:
        return jnp.dot(a, b, preferred_element_type=_F32)
    return lax.dot_general(a, b, dims, preferred_element_type=_F32)


def _rms(x, w):
    return x * lax.rsqrt(jnp.mean(x * x, axis=-1, keepdims=True) + EPS) * w


def _half_tanh_half(x):
    h = 0.5 * x
    return jnp.tanh(h), h


def _silu(x):
    t, h = _half_tanh_half(x)
    return h * t + h


def _gelu_exact(x):
    return 0.5 * x * (1.0 + lax.erf(x * math.sqrt(0.5)))


def _same_chunk_causal(rows, chunk):
    shift = chunk.bit_length() - 1
    assert chunk == 1 << shift
    t = lax.broadcasted_iota(jnp.int32, (rows, rows), 0)
    s = lax.broadcasted_iota(jnp.int32, (rows, rows), 1)
    same = lax.shift_right_logical(t, shift) == lax.shift_right_logical(s, shift)
    return jnp.logical_and(same, s <= t)


def _lane_cat(a, group):
    n = a.shape[0] // group
    if n == 1:
        return a
    return jnp.concatenate([a[gi * group:(gi + 1) * group] for gi in range(n)], axis=1)


def _lane_split(a, n):
    if n == 1:
        return a
    w = a.shape[1] // n
    return jnp.concatenate([a[:, gi * w:(gi + 1) * w] for gi in range(n)], axis=0)


def _chunk_cumsum(g, mask, group):
    n = g.shape[0] // group
    cum_mat = jnp.where(mask, 1.0, 0.0).astype(_BF16)
    g_hi = g.astype(_BF16)
    g_lo = (g - g_hi.astype(_F32)).astype(_BF16)
    return _lane_split(_dot(cum_mat, _lane_cat(g_hi, group)) + _dot(cum_mat, _lane_cat(g_lo, group)), n)


def _recurrence_fast(qs, kk, g, iv, side, *, states_t, group, chunk, carry_state):
    rows = qs.shape[0]
    n_groups = rows // group
    chunks_per_group = group // chunk
    n_chunks = rows // chunk
    blk_mask = _same_chunk_causal(group, chunk)
    b = _chunk_cumsum(g, blk_mask, group)
    side(1)

    iv = iv().astype(_BF16)
    q_dec = (qs * jnp.exp(b)).astype(_BF16)
    k_inv = kk * jnp.exp(-b)
    k_inv_bf = k_inv.astype(_BF16)

    low = None
    for c in range(n_chunks):
        b_last = b[(c + 1) * chunk - 1:(c + 1) * chunk, :]
        low = b_last if low is None else jnp.minimum(low, b_last)

    new_states = [[None] * H_A for _ in range(n_chunks)]
    o_heads = []
    for hd in range(H_A):
        ln = slice(hd * DK, (hd + 1) * DK)
        o_chunks = []
        s_t = states_t[0][hd]
        for gi in range(n_groups):
            rg = slice(gi * group, (gi + 1) * group)
            sc = _dot(q_dec[rg, ln], k_inv_bf[rg, ln], _NT)
            sc = jnp.where(blk_mask, sc, 0.0).astype(_BF16)
            intra = _dot(sc, iv[rg, ln])
            for ci in range(chunks_per_group):
                c = gi * chunks_per_group + ci
                rw = slice(c * chunk, (c + 1) * chunk)
                if not carry_state:
                    s_t = states_t[c][hd]
                inter = _dot(q_dec[rw, ln], s_t.astype(_BF16), _NT)
                o_chunks.append(inter + intra[ci * chunk:(ci + 1) * chunk])
                last = (c + 1) * chunk - 1
                dec_last = jnp.exp(b[last:last + 1, ln])
                k_upd = (k_inv[rw, ln] * dec_last).astype(_BF16)
                s_t = s_t * dec_last + _dot(iv[rw, ln], k_upd, _TN)
                new_states[c][hd] = s_t
        o_heads.append(jnp.concatenate(o_chunks, axis=0) if n_chunks > 1 else o_chunks[0])
        if hd == H_A // 2 - 1:
            side(2)
    return jnp.concatenate(o_heads, axis=1), (new_states, jnp.min(low))


def _recurrence_exact(qs, kk, g, iv, side, *, work_ref, state_ref, group, chunk):
    rows = qs.shape[0]
    assert chunk % SUBLANES == 0 and group % SUBLANES == 0
    minis_per_chunk = chunk // SUBLANES
    b = _chunk_cumsum(g, _same_chunk_causal(group, SUBLANES), group)
    for k, a in enumerate((qs, kk, b, iv())):
        work_ref[k, 0:rows, :] = a
    t_idx = lax.broadcasted_iota(jnp.int32, (SUBLANES, DK), 0)

    def mini(m, carry):
        r0 = pl.multiple_of(m * SUBLANES, SUBLANES)
        qs8, kk8, b8, iv8 = (work_ref[k, pl.ds(r0, SUBLANES), :] for k in range(4))
        c = m // minis_per_chunk
        for hd in range(H_A):
            ln = slice(hd * DK, (hd + 1) * DK)
            q_h, k_h, b_h, v_h = qs8[:, ln], kk8[:, ln], b8[:, ln], iv8[:, ln]
            s_t = state_ref[c, hd]
            acc = _dot(q_h * jnp.exp(b_h), s_t, _NT)
            for s in range(SUBLANES):
                dec = jnp.exp(jnp.minimum(b_h - b_h[s:s + 1, :], 0.0))
                term = jnp.where(t_idx >= s, q_h * k_h[s:s + 1, :] * dec, 0.0)
                acc = acc + jnp.sum(term, axis=1, keepdims=True) * v_h[s:s + 1, :]
            work_ref[4, pl.ds(r0, SUBLANES), ln] = acc
            b_end = b_h[SUBLANES - 1:SUBLANES, :]
            k_upd = k_h * jnp.exp(b_end - b_h)
            state_ref[c, hd] = s_t * jnp.exp(b_end) + _dot(v_h, k_upd, _TN)
        return carry

    lax.fori_loop(0, rows // SUBLANES, mini, 0)
    return work_ref[4, 0:rows, :], None


def _layer_block(x, refs, recurrence, *, group, mix_chunk):
    (norm_w_ref, w_in_ref, lb_logits_ref, g_norm_w_ref, ln_v_w_ref, ln_v_b_ref,
     w_s_ref, b_s_ref, w_out_ref, final_w_ref) = refs
    n_groups = x.shape[0] // group

    h = _rms(x, norm_w_ref[...]).astype(_BF16)
    slices = {}

    def proj(k):
        if k not in slices:
            slices[k] = _dot(h, w_in_ref[:, k * D_A:(k + 1) * D_A])
        return slices[k]

    def side(point):
        for k in PROJECTION_PLAN.get(point, ()):
            proj(k)

    logits = lb_logits_ref[...]
    e = jnp.exp(logits - jnp.max(logits, axis=0, keepdims=True))
    lb = e[0:1, :] / jnp.sum(e, axis=0, keepdims=True)

    side(0)
    half_span = 0.5 * (1.0 - lb)
    f = (lb + half_span) + half_span * _half_tanh_half(proj(1))[0]
    qs = _silu(proj(0)) * (DK ** -0.5)
    o, aux = recurrence(qs, 1.0 - f, jnp.log(f), lambda: proj(2), side)
    side(3)

    ga = proj(3)
    y_a_heads = []
    for hd in range(H_A):
        o_h = o[:, hd * DV:(hd + 1) * DV]
        y_a_heads.append(o_h * lax.rsqrt(jnp.mean(o_h * o_h, axis=-1, keepdims=True) + EPS))
    y_a = jnp.concatenate(y_a_heads, axis=1) * g_norm_w_ref[...] * _silu(ga)

    u = _gelu_exact(proj(4))
    v = _gelu_exact(proj(5))
    mu = jnp.mean(v, axis=-1, keepdims=True)
    vc = v - mu
    var = jnp.mean(vc * vc, axis=-1, keepdims=True)
    v_norm = vc * lax.rsqrt(var + EPS) * ln_v_w_ref[...] + ln_v_b_ref[...]
    gb = proj(6)
    v_bf = v_norm.astype(_BF16)
    mix_mask = _same_chunk_causal(group, mix_chunk)
    pos_r = lax.broadcasted_iota(jnp.int32, (group, MLP_CHUNK), 0)
    pos_c = lax.broadcasted_iota(jnp.int32, (group, MLP_CHUNK), 1)
    pos_mask = jnp.bitwise_and(pos_r, mix_chunk - 1) == pos_c
    mix_heads = []
    for hb in range(H_B):
        ln = slice(hb * C_B, (hb + 1) * C_B)
        if group == mix_chunk == MLP_CHUNK:
            w_full = w_s_ref[hb]
        else:
            w_rows = jnp.concatenate([w_s_ref[hb][:mix_chunk]] * (group // mix_chunk), axis=0)
            w_full = _dot(w_rows.astype(_BF16), jnp.where(pos_mask, 1.0, 0.0).astype(_BF16), _NT)
        w = jnp.where(mix_mask, w_full, 0.0).astype(_BF16)
        bias = jnp.sum(jnp.where(pos_mask, b_s_ref[hb:hb + 1, :], 0.0), axis=1, keepdims=True)
        mixed = _dot(w, _lane_cat(v_bf[:, ln], group)) + bias
        mix_heads.append(_lane_split(mixed, n_groups))
    mix = jnp.concatenate(mix_heads, axis=1)
    y_b = u * mix * _silu(gb)

    y = jnp.concatenate([y_a, y_b], axis=1).astype(_BF16)
    out = x + _dot(y, w_out_ref[...])
    return _rms(out, final_w_ref[...]), v_norm, aux


def _fused_kernel(xp_ref, xs_ref, state_in_ref, norm_w_ref, w_in_ref, lb_logits_ref, g_norm_w_ref,
                  ln_v_w_ref, ln_v_b_ref, w_s_ref, b_s_ref, w_out_ref, final_w_ref,
                  yp_ref, state_p_ref, ys_ref, state_s_ref, v_ref,
                  state_t_scr, state_prev_scr, work_scr,
                  *, prompt_steps, steps_per_stream, n_streams, frames):
    step = pl.program_id(0)
    refs = (norm_w_ref, w_in_ref, lb_logits_ref, g_norm_w_ref, ln_v_w_ref, ln_v_b_ref,
            w_s_ref, b_s_ref, w_out_ref, final_w_ref)

    def too_steep(low):
        return jnp.logical_not(low >= -LOG_DECAY_LIMIT)

    @pl.when(step < prompt_steps)
    def _prompt():
        j = lax.rem(step, steps_per_stream)

        @pl.when(j == 0)
        def _():
            state_t_scr[...] = jnp.zeros_like(state_t_scr)

        state_prev_scr[...] = state_t_scr[...]
        fast = functools.partial(
            _recurrence_fast, states_t=[[state_t_scr[0, hd] for hd in range(H_A)]],
            group=PROMPT_GROUP, chunk=PROMPT_CHUNK, carry_state=True)
        y, _, (new_states, low) = _layer_block(xp_ref[0], refs, fast, group=PROMPT_GROUP, mix_chunk=MLP_CHUNK)
        yp_ref[0] = y
        for hd in range(H_A):
            state_t_scr[0, hd] = new_states[-1][hd]

        @pl.when(too_steep(low))
        def _():
            state_t_scr[...] = state_prev_scr[...]
            exact = functools.partial(_recurrence_exact, work_ref=work_scr, state_ref=state_t_scr,
                                      group=PROMPT_GROUP, chunk=PROMPT_BLOCK)
            y_exact, _, _ = _layer_block(xp_ref[0], refs, exact, group=PROMPT_GROUP, mix_chunk=MLP_CHUNK)
            yp_ref[0] = y_exact

        @pl.when(j == steps_per_stream - 1)
        def _():
            for hd in range(H_A):
                state_p_ref[0, hd] = state_t_scr[0, hd].T

    @pl.when(step == prompt_steps)
    def _sample():
        rows = n_streams * frames
        fast = functools.partial(
            _recurrence_fast,
            states_t=[[state_in_ref[s, hd].T for hd in range(H_A)] for s in range(n_streams)],
            group=rows, chunk=frames, carry_state=False)
        y, v_norm, (new_states, low) = _layer_block(xs_ref[...], refs, fast, group=rows, mix_chunk=frames)
        ys_ref[...] = y
        v_ref[...] = v_norm
        for s in range(n_streams):
            for hd in range(H_A):
                state_s_ref[s, hd] = new_states[s][hd].T

        @pl.when(too_steep(low))
        def _():
            for s in range(n_streams):
                for hd in range(H_A):
                    state_s_ref[s, hd] = state_in_ref[s, hd].T
            exact = functools.partial(_recurrence_exact, work_ref=work_scr, state_ref=state_s_ref,
                                      group=rows, chunk=frames)
            y_exact, v_exact, _ = _layer_block(xs_ref[...], refs, exact, group=rows, mix_chunk=frames)
            ys_ref[...] = y_exact
            v_ref[...] = v_exact
            for s in range(n_streams):
                for hd in range(H_A):
                    state_s_ref[s, hd] = state_s_ref[s, hd].T


def _full(shape):
    return pl.BlockSpec(shape, lambda *_: (0,) * len(shape))


def kernel(x_prompt, x_sample, state_hgrn, norm_w, w_in, lb_logits, g_norm_w, ln_v_w, ln_v_b, w_s, b_s, w_out, final_norm_w):
    batch, seq, _ = x_prompt.shape
    dec_batch, dec_seq, _ = x_sample.shape
    assert norm_w.shape[0] == 1, "single-layer trunk"
    assert seq % PROMPT_BLOCK == 0 and PROMPT_BLOCK % PROMPT_GROUP == 0
    assert PROMPT_GROUP % MLP_CHUNK == 0 and MLP_CHUNK % PROMPT_CHUNK == 0
    assert dec_seq <= HGRN_CHUNK and dec_seq <= MLP_CHUNK and dec_seq & (dec_seq - 1) == 0
    assert w_s.shape[2:] == (MLP_CHUNK, MLP_CHUNK)

    rp = PROMPT_BLOCK
    rs = dec_batch * dec_seq
    steps_per_stream = seq // rp
    prompt_steps = batch * steps_per_stream
    row = lambda a: a.reshape(1, -1)

    def prompt_block(step):
        blk = jnp.minimum(step, prompt_steps - 1)
        return blk // steps_per_stream, blk % steps_per_stream

    def x_map(step):
        b, j = prompt_block(step)
        return b, j, 0

    def state_map(step):
        b, _ = prompt_block(step)
        return b, 0, 0, 0

    y_prompt, state_p, y_sample, state_s, v_sample = pl.pallas_call(
        functools.partial(_fused_kernel, prompt_steps=prompt_steps, steps_per_stream=steps_per_stream,
                          n_streams=dec_batch, frames=dec_seq),
        grid=(prompt_steps + 1,),
        in_specs=[pl.BlockSpec((1, rp, D_MODEL), x_map),
                  _full((rs, D_MODEL)), _full((dec_batch, H_A, DK, DV)),
                  _full((1, D_MODEL)), _full((D_MODEL, IN_WIDTH)), _full(lb_logits.shape),
                  _full((1, D_A)), _full((1, D_B)), _full((1, D_B)),
                  _full((H_B, MLP_CHUNK, MLP_CHUNK)), _full((H_B, MLP_CHUNK)),
                  _full((D_MODEL, D_MODEL)), _full((1, D_MODEL))],
        out_specs=[pl.BlockSpec((1, rp, D_MODEL), x_map),
                   pl.BlockSpec((1, H_A, DK, DV), state_map),
                   _full((rs, D_MODEL)), _full((dec_batch, H_A, DK, DV)), _full((rs, D_B))],
        out_shape=[jax.ShapeDtypeStruct((batch, seq, D_MODEL), _F32),
                   jax.ShapeDtypeStruct((batch, H_A, DK, DV), _F32),
                   jax.ShapeDtypeStruct((rs, D_MODEL), _F32),
                   jax.ShapeDtypeStruct((dec_batch, H_A, DK, DV), _F32),
                   jax.ShapeDtypeStruct((rs, D_B), _F32)],
        scratch_shapes=[pltpu.VMEM((1, H_A, DV, DK), _F32),
                        pltpu.VMEM((1, H_A, DV, DK), _F32),
                        pltpu.VMEM((5, max(rp, rs), D_A), _F32)],
        compiler_params=pltpu.CompilerParams(
            dimension_semantics=("arbitrary",), vmem_limit_bytes=VMEM_LIMIT_BYTES),
        name="layer_step",
    )(x_prompt, x_sample.reshape(rs, D_MODEL), state_hgrn[0],
      row(norm_w[0]), w_in[0].astype(_BF16), lb_logits, row(g_norm_w[0]), row(ln_v_w[0]), row(ln_v_b[0]),
      w_s[0], b_s[0], w_out[0].astype(_BF16), row(final_norm_w))

    return (y_prompt,
            y_sample.reshape(dec_batch, dec_seq, D_MODEL),
            state_p[None],
            state_s[None],
            v_sample.reshape(1, dec_batch, dec_seq, D_B))
```

```python
import functools
import math

import jax
import jax.numpy as jnp
from jax import lax
from jax.experimental import pallas as pl
from jax.experimental.pallas import tpu as pltpu

D_MODEL = 1024
D_A = 512
D_B = 512
H_A = 4
DK = 128
DV = 128
H_B = 4
C_B = 128
MLP_CHUNK = 128
HGRN_CHUNK = 64
IN_WIDTH = 4 * D_A + 3 * D_B
EPS = 1e-6

PROMPT_BLOCK = 512
PROMPT_GROUP = 128
PROMPT_CHUNK = 128
SUBLANES = 8
LOG_DECAY_LIMIT = 78.0
VMEM_LIMIT_BYTES = 56 * 1024 * 1024
PROJECTION_PLAN = {0: (1, 0), 1: (2, 4), 2: (5,), 3: (3, 6)}

WEIGHT_STAGE_COLS = 512

_NT = (((1,), (1,)), ((), ()))
_TN = (((0,), (0,)), ((), ()))
_F32 = jnp.float32
_BF16 = jnp.bfloat16


def _dot(a, b, dims=None):
    if dims is None:
        return jnp.dot(a, b, preferred_element_type=_F32)
    return lax.dot_general(a, b, dims, preferred_element_type=_F32)


def _rms(x, w):
    return x * lax.rsqrt(jnp.mean(x * x, axis=-1, keepdims=True) + EPS) * w


def _half_tanh_half(x):
    h = 0.5 * x
    return jnp.tanh(h), h


def _silu(x):
    t, h = _half_tanh_half(x)
    return h * t + h


def _gelu_exact(x):
    return 0.5 * x * (1.0 + lax.erf(x * math.sqrt(0.5)))


def _same_chunk_causal(rows, chunk):
    shift = chunk.bit_length() - 1
    assert chunk == 1 << shift
    t = lax.broadcasted_iota(jnp.int32, (rows, rows), 0)
    s = lax.broadcasted_iota(jnp.int32, (rows, rows), 1)
    same = lax.shift_right_logical(t, shift) == lax.shift_right_logical(s, shift)
    return jnp.logical_and(same, s <= t)


def _lane_cat(a, group):
    n = a.shape[0] // group
    if n == 1:
        return a
    return jnp.concatenate([a[gi * group:(gi + 1) * group] for gi in range(n)], axis=1)


def _lane_split(a, n):
    if n == 1:
        return a
    w = a.shape[1] // n
    return jnp.concatenate([a[:, gi * w:(gi + 1) * w] for gi in range(n)], axis=0)


def _chunk_cumsum(g, mask, group):
    n = g.shape[0] // group
    cum_mat = jnp.where(mask, 1.0, 0.0).astype(_BF16)
    g_hi = g.astype(_BF16)
    g_lo = (g - g_hi.astype(_F32)).astype(_BF16)
    return _lane_split(_dot(cum_mat, _lane_cat(g_hi, group)) + _dot(cum_mat, _lane_cat(g_lo, group)), n)


def _recurrence_fast(qs, kk, g, iv, side, *, states_t, group, chunk, carry_state):
    rows = qs.shape[0]
    n_groups = rows // group
    chunks_per_group = group // chunk
    n_chunks = rows // chunk
    blk_mask = _same_chunk_causal(group, chunk)
    b = _chunk_cumsum(g, blk_mask, group)
    side(1)

    iv = iv().astype(_BF16)
    q_dec = (qs * jnp.exp(b)).astype(_BF16)
    k_inv = kk * jnp.exp(-b)
    k_inv_bf = k_inv.astype(_BF16)

    low = None
    for c in range(n_chunks):
        b_last = b[(c + 1) * chunk - 1:(c + 1) * chunk, :]
        low = b_last if low is None else jnp.minimum(low, b_last)

    new_states = [[None] * H_A for _ in range(n_chunks)]
    o_heads = []
    for hd in range(H_A):
        ln = slice(hd * DK, (hd + 1) * DK)
        o_chunks = []
        s_t = states_t[0][hd]
        for gi in range(n_groups):
            rg = slice(gi * group, (gi + 1) * group)
            sc = _dot(q_dec[rg, ln], k_inv_bf[rg, ln], _NT)
            sc = jnp.where(blk_mask, sc, 0.0).astype(_BF16)
            intra = _dot(sc, iv[rg, ln])
            for ci in range(chunks_per_group):
                c = gi * chunks_per_group + ci
                rw = slice(c * chunk, (c + 1) * chunk)
                if not carry_state:
                    s_t = states_t[c][hd]
                inter = _dot(q_dec[rw, ln], s_t.astype(_BF16), _NT)
                o_chunks.append(inter + intra[ci * chunk:(ci + 1) * chunk])
                last = (c + 1) * chunk - 1
                dec_last = jnp.exp(b[last:last + 1, ln])
                k_upd = (k_inv[rw, ln] * dec_last).astype(_BF16)
                s_t = s_t * dec_last + _dot(iv[rw, ln], k_upd, _TN)
                new_states[c][hd] = s_t
        o_heads.append(jnp.concatenate(o_chunks, axis=0) if n_chunks > 1 else o_chunks[0])
        if hd == H_A // 2 - 1:
            side(2)
    return jnp.concatenate(o_heads, axis=1), (new_states, jnp.min(low))


def _recurrence_exact(qs, kk, g, iv, side, *, work_ref, state_ref, group, chunk):
    rows = qs.shape[0]
    assert chunk % SUBLANES == 0 and group % SUBLANES == 0
    minis_per_chunk = chunk // SUBLANES
    b = _chunk_cumsum(g, _same_chunk_causal(group, SUBLANES), group)
    for k, a in enumerate((qs, kk, b, iv())):
        work_ref[k, 0:rows, :] = a
    t_idx = lax.broadcasted_iota(jnp.int32, (SUBLANES, DK), 0)

    def mini(m, carry):
        r0 = pl.multiple_of(m * SUBLANES, SUBLANES)
        qs8, kk8, b8, iv8 = (work_ref[k, pl.ds(r0, SUBLANES), :] for k in range(4))
        c = m // minis_per_chunk
        for hd in range(H_A):
            ln = slice(hd * DK, (hd + 1) * DK)
            q_h, k_h, b_h, v_h = qs8[:, ln], kk8[:, ln], b8[:, ln], iv8[:, ln]
            s_t = state_ref[c, hd]
            acc = _dot(q_h * jnp.exp(b_h), s_t, _NT)
            for s in range(SUBLANES):
                dec = jnp.exp(jnp.minimum(b_h - b_h[s:s + 1, :], 0.0))
                term = jnp.where(t_idx >= s, q_h * k_h[s:s + 1, :] * dec, 0.0)
                acc = acc + jnp.sum(term, axis=1, keepdims=True) * v_h[s:s + 1, :]
            work_ref[4, pl.ds(r0, SUBLANES), ln] = acc
            b_end = b_h[SUBLANES - 1:SUBLANES, :]
            k_upd = k_h * jnp.exp(b_end - b_h)
            state_ref[c, hd] = s_t * jnp.exp(b_end) + _dot(v_h, k_upd, _TN)
        return carry

    lax.fori_loop(0, rows // SUBLANES, mini, 0)
    return work_ref[4, 0:rows, :], None


def _layer_block(x, refs, recurrence, *, group, mix_chunk):
    (norm_w_ref, w_in_ref, lb_logits_ref, g_norm_w_ref, ln_v_w_ref, ln_v_b_ref,
     w_s_ref, b_s_ref, w_out_ref, final_w_ref) = refs
    n_groups = x.shape[0] // group

    h = _rms(x, norm_w_ref[...]).astype(_BF16)
    slices = {}

    def proj(k):
        if k not in slices:
            slices[k] = _dot(h, w_in_ref[:, k * D_A:(k + 1) * D_A])
        return slices[k]

    def side(point):
        for k in PROJECTION_PLAN.get(point, ()):
            proj(k)

    logits = lb_logits_ref[...]
    e = jnp.exp(logits - jnp.max(logits, axis=0, keepdims=True))
    lb = e[0:1, :] / jnp.sum(e, axis=0, keepdims=True)

    side(0)
    half_span = 0.5 * (1.0 - lb)
    f = (lb + half_span) + half_span * _half_tanh_half(proj(1))[0]
    qs = _silu(proj(0)) * (DK ** -0.5)
    o, aux = recurrence(qs, 1.0 - f, jnp.log(f), lambda: proj(2), side)
    side(3)

    ga = proj(3)
    y_a_heads = []
    for hd in range(H_A):
        o_h = o[:, hd * DV:(hd + 1) * DV]
        y_a_heads.append(o_h * lax.rsqrt(jnp.mean(o_h * o_h, axis=-1, keepdims=True) + EPS))
    y_a = jnp.concatenate(y_a_heads, axis=1) * g_norm_w_ref[...] * _silu(ga)

    u = _gelu_exact(proj(4))
    v = _gelu_exact(proj(5))
    mu = jnp.mean(v, axis=-1, keepdims=True)
    vc = v - mu
    var = jnp.mean(vc * vc, axis=-1, keepdims=True)
    v_norm = vc * lax.rsqrt(var + EPS) * ln_v_w_ref[...] + ln_v_b_ref[...]
    gb = proj(6)
    v_bf = v_norm.astype(_BF16)
    mix_mask = _same_chunk_causal(group, mix_chunk)
    pos_r = lax.broadcasted_iota(jnp.int32, (group, MLP_CHUNK), 0)
    pos_c = lax.broadcasted_iota(jnp.int32, (group, MLP_CHUNK), 1)
    pos_mask = jnp.bitwise_and(pos_r, mix_chunk - 1) == pos_c
    mix_heads = []
    for hb in range(H_B):
        ln = slice(hb * C_B, (hb + 1) * C_B)
        if group == mix_chunk == MLP_CHUNK:
            w_full = w_s_ref[hb]
        else:
            w_rows = jnp.concatenate([w_s_ref[hb][:mix_chunk]] * (group // mix_chunk), axis=0)
            w_full = _dot(w_rows.astype(_BF16), jnp.where(pos_mask, 1.0, 0.0).astype(_BF16), _NT)
        w = jnp.where(mix_mask, w_full, 0.0).astype(_BF16)
        bias = jnp.sum(jnp.where(pos_mask, b_s_ref[hb:hb + 1, :], 0.0), axis=1, keepdims=True)
        mixed = _dot(w, _lane_cat(v_bf[:, ln], group)) + bias
        mix_heads.append(_lane_split(mixed, n_groups))
    mix = jnp.concatenate(mix_heads, axis=1)
    y_b = u * mix * _silu(gb)

    y = jnp.concatenate([y_a, y_b], axis=1).astype(_BF16)
    out = x + _dot(y, w_out_ref[...])
    return _rms(out, final_w_ref[...]), v_norm, aux


def _stage_weights_bf16(pairs, stage_ref, sem_ref):
    chunks = [(src, dst, c0) for src, dst in pairs for c0 in range(0, src.shape[1], WEIGHT_STAGE_COLS)]

    def copy(i):
        src, _, c0 = chunks[i]
        return pltpu.make_async_copy(src.at[:, c0:c0 + WEIGHT_STAGE_COLS], stage_ref.at[i % 2], sem_ref.at[i % 2])

    copy(0).start()
    for i, (_, dst, c0) in enumerate(chunks):
        if i + 1 < len(chunks):
            copy(i + 1).start()
        copy(i).wait()
        dst[:, c0:c0 + WEIGHT_STAGE_COLS] = stage_ref[i % 2].astype(_BF16)


def _fused_kernel(xp_ref, xs_ref, state_in_ref, norm_w_ref, w_in_hbm, lb_logits_ref, g_norm_w_ref,
                  ln_v_w_ref, ln_v_b_ref, w_s_ref, b_s_ref, w_out_hbm, final_w_ref,
                  yp_ref, state_p_ref, ys_ref, state_s_ref, v_ref,
                  state_t_scr, state_prev_scr, work_scr, w_in_ref, w_out_ref, stage_scr, stage_sem,
                  *, prompt_steps, steps_per_stream, n_streams, frames):
    step = pl.program_id(0)
    refs = (norm_w_ref, w_in_ref, lb_logits_ref, g_norm_w_ref, ln_v_w_ref, ln_v_b_ref,
            w_s_ref, b_s_ref, w_out_ref, final_w_ref)

    @pl.when(step == 0)
    def _():
        _stage_weights_bf16([(w_in_hbm, w_in_ref), (w_out_hbm, w_out_ref)], stage_scr, stage_sem)

    def too_steep(low):
        return jnp.logical_not(low >= -LOG_DECAY_LIMIT)

    @pl.when(step < prompt_steps)
    def _prompt():
        j = lax.rem(step, steps_per_stream)

        @pl.when(j == 0)
        def _():
            state_t_scr[...] = jnp.zeros_like(state_t_scr)

        state_prev_scr[...] = state_t_scr[...]
        fast = functools.partial(
            _recurrence_fast, states_t=[[state_t_scr[0, hd] for hd in range(H_A)]],
            group=PROMPT_GROUP, chunk=PROMPT_CHUNK, carry_state=True)
        y, _, (new_states, low) = _layer_block(xp_ref[0], refs, fast, group=PROMPT_GROUP, mix_chunk=MLP_CHUNK)
        yp_ref[0] = y
        for hd in range(H_A):
            state_t_scr[0, hd] = new_states[-1][hd]

        @pl.when(too_steep(low))
        def _():
            state_t_scr[...] = state_prev_scr[...]
            exact = functools.partial(_recurrence_exact, work_ref=work_scr, state_ref=state_t_scr,
                                      group=PROMPT_GROUP, chunk=PROMPT_BLOCK)
            y_exact, _, _ = _layer_block(xp_ref[0], refs, exact, group=PROMPT_GROUP, mix_chunk=MLP_CHUNK)
            yp_ref[0] = y_exact

        @pl.when(j == steps_per_stream - 1)
        def _():
            for hd in range(H_A):
                state_p_ref[0, hd] = state_t_scr[0, hd].T

    @pl.when(step == prompt_steps)
    def _sample():
        rows = n_streams * frames
        fast = functools.partial(
            _recurrence_fast,
            states_t=[[state_in_ref[s, hd].T for hd in range(H_A)] for s in range(n_streams)],
            group=rows, chunk=frames, carry_state=False)
        y, v_norm, (new_states, low) = _layer_block(xs_ref[...], refs, fast, group=rows, mix_chunk=frames)
        ys_ref[...] = y
        v_ref[...] = v_norm
        for s in range(n_streams):
            for hd in range(H_A):
                state_s_ref[s, hd] = new_states[s][hd].T

        @pl.when(too_steep(low))
        def _():
            for s in range(n_streams):
                for hd in range(H_A):
                    state_s_ref[s, hd] = state_in_ref[s, hd].T
            exact = functools.partial(_recurrence_exact, work_ref=work_scr, state_ref=state_s_ref,
                                      group=rows, chunk=frames)
            y_exact, v_exact, _ = _layer_block(xs_ref[...], refs, exact, group=rows, mix_chunk=frames)
            ys_ref[...] = y_exact
            v_ref[...] = v_exact
            for s in range(n_streams):
                for hd in range(H_A):
                    state_s_ref[s, hd] = state_s_ref[s, hd].T


def _full(shape):
    return pl.BlockSpec(shape, lambda *_: (0,) * len(shape))


def kernel(x_prompt, x_sample, state_hgrn, norm_w, w_in, lb_logits, g_norm_w, ln_v_w, ln_v_b, w_s, b_s, w_out, final_norm_w):
    batch, seq, _ = x_prompt.shape
    dec_batch, dec_seq, _ = x_sample.shape
    assert norm_w.shape[0] == 1, "single-layer trunk"
    assert seq % PROMPT_BLOCK == 0 and PROMPT_BLOCK % PROMPT_GROUP == 0
    assert PROMPT_GROUP % MLP_CHUNK == 0 and MLP_CHUNK % PROMPT_CHUNK == 0
    assert dec_seq <= HGRN_CHUNK and dec_seq <= MLP_CHUNK and dec_seq & (dec_seq - 1) == 0
    assert w_s.shape[2:] == (MLP_CHUNK, MLP_CHUNK)

    rp = PROMPT_BLOCK
    rs = dec_batch * dec_seq
    steps_per_stream = seq // rp
    prompt_steps = batch * steps_per_stream
    row = lambda a: a.reshape(1, -1)

    def prompt_block(step):
        blk = jnp.minimum(step, prompt_steps - 1)
        return blk // steps_per_stream, blk % steps_per_stream

    def x_map(step):
        b, j = prompt_block(step)
        return b, j, 0

    def state_map(step):
        b, _ = prompt_block(step)
        return b, 0, 0, 0

    y_prompt, state_p, y_sample, state_s, v_sample = pl.pallas_call(
        functools.partial(_fused_kernel, prompt_steps=prompt_steps, steps_per_stream=steps_per_stream,
                          n_streams=dec_batch, frames=dec_seq),
        grid=(prompt_steps + 1,),
        in_specs=[pl.BlockSpec((1, rp, D_MODEL), x_map),
                  _full((rs, D_MODEL)), _full((dec_batch, H_A, DK, DV)),
                  _full((1, D_MODEL)), pl.BlockSpec(memory_space=pl.ANY), _full(lb_logits.shape),
                  _full((1, D_A)), _full((1, D_B)), _full((1, D_B)),
                  _full((H_B, MLP_CHUNK, MLP_CHUNK)), _full((H_B, MLP_CHUNK)),
                  pl.BlockSpec(memory_space=pl.ANY), _full((1, D_MODEL))],
        out_specs=[pl.BlockSpec((1, rp, D_MODEL), x_map),
                   pl.BlockSpec((1, H_A, DK, DV), state_map),
                   _full((rs, D_MODEL)), _full((dec_batch, H_A, DK, DV)), _full((rs, D_B))],
        out_shape=[jax.ShapeDtypeStruct((batch, seq, D_MODEL), _F32),
                   jax.ShapeDtypeStruct((batch, H_A, DK, DV), _F32),
                   jax.ShapeDtypeStruct((rs, D_MODEL), _F32),
                   jax.ShapeDtypeStruct((dec_batch, H_A, DK, DV), _F32),
                   jax.ShapeDtypeStruct((rs, D_B), _F32)],
        scratch_shapes=[pltpu.VMEM((1, H_A, DV, DK), _F32),
                        pltpu.VMEM((1, H_A, DV, DK), _F32),
                        pltpu.VMEM((5, max(rp, rs), D_A), _F32),
                        pltpu.VMEM((D_MODEL, IN_WIDTH), _BF16),
                        pltpu.VMEM((D_MODEL, D_MODEL), _BF16),
                        pltpu.VMEM((2, D_MODEL, WEIGHT_STAGE_COLS), _F32),
                        pltpu.SemaphoreType.DMA((2,))],
        compiler_params=pltpu.CompilerParams(
            dimension_semantics=("arbitrary",), vmem_limit_bytes=VMEM_LIMIT_BYTES),
        name="layer_step",
    )(x_prompt, x_sample.reshape(rs, D_MODEL), state_hgrn[0],
      row(norm_w[0]), w_in.reshape(D_MODEL, IN_WIDTH), lb_logits, row(g_norm_w[0]), row(ln_v_w[0]), row(ln_v_b[0]),
      w_s[0], b_s[0], w_out.reshape(D_MODEL, D_MODEL), row(final_norm_w))

    return (y_prompt,
            y_sample.reshape(dec_batch, dec_seq, D_MODEL),
            state_p[None],
            state_s[None],
            v_sample.reshape(1, dec_batch, dec_seq, D_B))
```

```python
import functools
import math

import jax
import jax.numpy as jnp
from jax import lax
from jax.experimental import pallas as pl
from jax.experimental.pallas import tpu as pltpu

D_MODEL = 1024
D_A = 512
D_B = 512
H_A = 4
DK = 128
DV = 128
H_B = 4
C_B = 128
MLP_CHUNK = 128
HGRN_CHUNK = 64
IN_WIDTH = 4 * D_A + 3 * D_B
EPS = 1e-6

PROMPT_BLOCK = 512
PROMPT_GROUP = 128
PROMPT_CHUNK = 128
SUBLANES = 8
LOG_DECAY_LIMIT = 78.0
VMEM_LIMIT_BYTES = 56 * 1024 * 1024
PROJECTION_PLAN = {0: (1, 0), 1: (2, 4), 2: (5,), 3: (3, 6)}

WEIGHT_STAGE_COLS = 512
OUT_PARTS = 2

_NT = (((1,), (1,)), ((), ()))
_TN = (((0,), (0,)), ((), ()))
_F32 = jnp.float32
_BF16 = jnp.bfloat16


def _dot(a, b, dims=None):
    if dims is None:
        return jnp.dot(a, b, preferred_element_type=_F32)
    return lax.dot_general(a, b, dims, preferred_element_type=_F32)


def _rms(x, w):
    return x * lax.rsqrt(jnp.mean(x * x, axis=-1, keepdims=True) + EPS) * w


def _half_tanh_half(x):
    h = 0.5 * x
    return jnp.tanh(h), h


def _silu(x):
    t, h = _half_tanh_half(x)
    return h * t + h


def _gelu_exact(x):
    return 0.5 * x * (1.0 + lax.erf(x * math.sqrt(0.5)))


def _same_chunk_causal(rows, chunk):
    shift = chunk.bit_length() - 1
    assert chunk == 1 << shift
    t = lax.broadcasted_iota(jnp.int32, (rows, rows), 0)
    s = lax.broadcasted_iota(jnp.int32, (rows, rows), 1)
    same = lax.shift_right_logical(t, shift) == lax.shift_right_logical(s, shift)
    return jnp.logical_and(same, s <= t)


def _lane_cat(a, group):
    n = a.shape[0] // group
    if n == 1:
        return a
    return jnp.concatenate([a[gi * group:(gi + 1) * group] for gi in range(n)], axis=1)


def _lane_split(a, n):
    if n == 1:
        return a
    w = a.shape[1] // n
    return jnp.concatenate([a[:, gi * w:(gi + 1) * w] for gi in range(n)], axis=0)


def _chunk_cumsum(g, chunk):
    rows, width = g.shape
    n_regs = rows // SUBLANES
    x = g.reshape(n_regs, SUBLANES, width)
    sub = lax.broadcasted_iota(jnp.int32, x.shape, 1)
    shift = 1
    while shift < SUBLANES:
        x = x + jnp.where(sub >= shift, pltpu.roll(x, shift, 1), 0.0)
        shift *= 2
    regs_per_chunk = chunk // SUBLANES
    if regs_per_chunk > 1:
        parts, run = [], None
        for r in range(n_regs):
            part = x[r] + run if r % regs_per_chunk else x[r]
            run = part[SUBLANES - 1:SUBLANES, :]
            parts.append(part)
        return jnp.concatenate(parts, axis=0)
    return x.reshape(rows, width)


def _recurrence_fast(qs, kk, g, iv, side, *, states_t, group, chunk, carry_state):
    rows = qs.shape[0]
    n_groups = rows // group
    chunks_per_group = group // chunk
    n_chunks = rows // chunk
    blk_mask = _same_chunk_causal(group, chunk)
    b = _chunk_cumsum(g, chunk)
    side(1)

    iv = iv().astype(_BF16)
    q_dec = (qs * jnp.exp(b)).astype(_BF16)
    k_inv = kk * jnp.exp(-b)
    k_inv_bf = k_inv.astype(_BF16)

    low = None
    for c in range(n_chunks):
        b_last = b[(c + 1) * chunk - 1:(c + 1) * chunk, :]
        low = b_last if low is None else jnp.minimum(low, b_last)

    new_states = [[None] * H_A for _ in range(n_chunks)]
    o_heads = []
    for hd in range(H_A):
        ln = slice(hd * DK, (hd + 1) * DK)
        o_chunks = []
        s_t = states_t[0][hd]
        for gi in range(n_groups):
            rg = slice(gi * group, (gi + 1) * group)
            sc = _dot(q_dec[rg, ln], k_inv_bf[rg, ln], _NT)
            sc = jnp.where(blk_mask, sc, 0.0).astype(_BF16)
            intra = _dot(sc, iv[rg, ln])
            for ci in range(chunks_per_group):
                c = gi * chunks_per_group + ci
                rw = slice(c * chunk, (c + 1) * chunk)
                if not carry_state:
                    s_t = states_t[c][hd]
                inter = _dot(q_dec[rw, ln], s_t.astype(_BF16), _NT)
                o_chunks.append(inter + intra[ci * chunk:(ci + 1) * chunk])
                last = (c + 1) * chunk - 1
                dec_last = jnp.exp(b[last:last + 1, ln])
                k_upd = (k_inv[rw, ln] * dec_last).astype(_BF16)
                s_t = s_t * dec_last + _dot(iv[rw, ln], k_upd, _TN)
                new_states[c][hd] = s_t
        o_heads.append(jnp.concatenate(o_chunks, axis=0) if n_chunks > 1 else o_chunks[0])
        if hd == H_A // 2 - 1:
            side(2)
    return jnp.concatenate(o_heads, axis=1), (new_states, jnp.min(low))


def _recurrence_exact(qs, kk, g, iv, side, *, work_ref, state_ref, group, chunk):
    rows = qs.shape[0]
    assert chunk % SUBLANES == 0 and group % SUBLANES == 0
    minis_per_chunk = chunk // SUBLANES
    b = _chunk_cumsum(g, SUBLANES)
    for k, a in enumerate((qs, kk, b, iv())):
        work_ref[k, 0:rows, :] = a
    t_idx = lax.broadcasted_iota(jnp.int32, (SUBLANES, DK), 0)

    def mini(m, carry):
        r0 = pl.multiple_of(m * SUBLANES, SUBLANES)
        qs8, kk8, b8, iv8 = (work_ref[k, pl.ds(r0, SUBLANES), :] for k in range(4))
        c = m // minis_per_chunk
        for hd in range(H_A):
            ln = slice(hd * DK, (hd + 1) * DK)
            q_h, k_h, b_h, v_h = qs8[:, ln], kk8[:, ln], b8[:, ln], iv8[:, ln]
            s_t = state_ref[c, hd]
            acc = _dot(q_h * jnp.exp(b_h), s_t, _NT)
            for s in range(SUBLANES):
                dec = jnp.exp(jnp.minimum(b_h - b_h[s:s + 1, :], 0.0))
                term = jnp.where(t_idx >= s, q_h * k_h[s:s + 1, :] * dec, 0.0)
                acc = acc + jnp.sum(term, axis=1, keepdims=True) * v_h[s:s + 1, :]
            work_ref[4, pl.ds(r0, SUBLANES), ln] = acc
            b_end = b_h[SUBLANES - 1:SUBLANES, :]
            k_upd = k_h * jnp.exp(b_end - b_h)
            state_ref[c, hd] = s_t * jnp.exp(b_end) + _dot(v_h, k_upd, _TN)
        return carry

    lax.fori_loop(0, rows // SUBLANES, mini, 0)
    return work_ref[4, 0:rows, :], None


def _layer_block(x, refs, recurrence, *, group, mix_chunk):
    (norm_w_ref, w_in_ref, lb_logits_ref, g_norm_w_ref, ln_v_w_ref, ln_v_b_ref,
     w_s_ref, b_s_ref, w_out_ref, final_w_ref) = refs
    n_groups = x.shape[0] // group

    h = _rms(x, norm_w_ref[...]).astype(_BF16)
    slices = {}

    def proj(k):
        if k not in slices:
            slices[k] = _dot(h, w_in_ref[:, k * D_A:(k + 1) * D_A])
        return slices[k]

    def side(point):
        for k in PROJECTION_PLAN.get(point, ()):
            proj(k)

    logits = lb_logits_ref[...]
    e = jnp.exp(logits - jnp.max(logits, axis=0, keepdims=True))
    lb = e[0:1, :] / jnp.sum(e, axis=0, keepdims=True)

    side(0)
    half_span = 0.5 * (1.0 - lb)
    f = (lb + half_span) + half_span * _half_tanh_half(proj(1))[0]
    qs = _silu(proj(0)) * (DK ** -0.5)
    o, aux = recurrence(qs, 1.0 - f, jnp.log(f), lambda: proj(2), side)
    side(3)

    ga = proj(3)
    y_a_heads = []
    for hd in range(H_A):
        o_h = o[:, hd * DV:(hd + 1) * DV]
        y_a_heads.append(o_h * lax.rsqrt(jnp.mean(o_h * o_h, axis=-1, keepdims=True) + EPS))
    y_a = jnp.concatenate(y_a_heads, axis=1) * g_norm_w_ref[...] * _silu(ga)

    u = _gelu_exact(proj(4))
    v = _gelu_exact(proj(5))
    mu = jnp.mean(v, axis=-1, keepdims=True)
    vc = v - mu
    var = jnp.mean(vc * vc, axis=-1, keepdims=True)
    v_norm = vc * lax.rsqrt(var + EPS) * ln_v_w_ref[...] + ln_v_b_ref[...]
    gb = proj(6)
    v_bf = v_norm.astype(_BF16)
    mix_mask = _same_chunk_causal(group, mix_chunk)
    pos_r = lax.broadcasted_iota(jnp.int32, (group, MLP_CHUNK), 0)
    pos_c = lax.broadcasted_iota(jnp.int32, (group, MLP_CHUNK), 1)
    pos_mask = jnp.bitwise_and(pos_r, mix_chunk - 1) == pos_c
    mix_heads = []
    for hb in range(H_B):
        ln = slice(hb * C_B, (hb + 1) * C_B)
        if group == mix_chunk == MLP_CHUNK:
            w_full = w_s_ref[hb]
        else:
            w_rows = jnp.concatenate([w_s_ref[hb][:mix_chunk]] * (group // mix_chunk), axis=0)
            w_full = _dot(w_rows.astype(_BF16), jnp.where(pos_mask, 1.0, 0.0).astype(_BF16), _NT)
        w = jnp.where(mix_mask, w_full, 0.0).astype(_BF16)
        bias = jnp.sum(jnp.where(pos_mask, b_s_ref[hb:hb + 1, :], 0.0), axis=1, keepdims=True)
        mixed = _dot(w, _lane_cat(v_bf[:, ln], group)) + bias
        mix_heads.append(_lane_split(mixed, n_groups))
    mix = jnp.concatenate(mix_heads, axis=1)
    y_b = u * mix * _silu(gb)

    y = jnp.concatenate([y_a, y_b], axis=1).astype(_BF16)
    part = max(x.shape[0] // OUT_PARTS, group)
    outs = []
    for r0 in range(0, x.shape[0], part):
        out = x[r0:r0 + part] + _dot(y[r0:r0 + part], w_out_ref[...])
        outs.append(_rms(out, final_w_ref[...]))
    return jnp.concatenate(outs, axis=0), v_norm, aux


def _stage_weights_bf16(pairs, stage_ref, sem_ref):
    chunks = [(src, dst, c0) for src, dst in pairs for c0 in range(0, src.shape[1], WEIGHT_STAGE_COLS)]

    def copy(i):
        src, _, c0 = chunks[i]
        return pltpu.make_async_copy(src.at[:, c0:c0 + WEIGHT_STAGE_COLS], stage_ref.at[i % 2], sem_ref.at[i % 2])

    copy(0).start()
    for i, (_, dst, c0) in enumerate(chunks):
        if i + 1 < len(chunks):
            copy(i + 1).start()
        copy(i).wait()
        dst[:, c0:c0 + WEIGHT_STAGE_COLS] = stage_ref[i % 2].astype(_BF16)


def _fused_kernel(xp_ref, xs_ref, state_in_ref, norm_w_ref, w_in_hbm, lb_logits_ref, g_norm_w_ref,
                  ln_v_w_ref, ln_v_b_ref, w_s_ref, b_s_ref, w_out_hbm, final_w_ref,
                  yp_ref, state_p_ref, ys_ref, state_s_ref, v_ref,
                  state_t_scr, state_prev_scr, work_scr, w_in_ref, w_out_ref, stage_scr, stage_sem,
                  *, prompt_steps, steps_per_stream, n_streams, frames):
    step = pl.program_id(0)
    refs = (norm_w_ref, w_in_ref, lb_logits_ref, g_norm_w_ref, ln_v_w_ref, ln_v_b_ref,
            w_s_ref, b_s_ref, w_out_ref, final_w_ref)

    @pl.when(step == 0)
    def _():
        _stage_weights_bf16([(w_in_hbm, w_in_ref), (w_out_hbm, w_out_ref)], stage_scr, stage_sem)

    def too_steep(low):
        return jnp.logical_not(low >= -LOG_DECAY_LIMIT)

    @pl.when(step < prompt_steps)
    def _prompt():
        j = lax.rem(step, steps_per_stream)

        @pl.when(j == 0)
        def _():
            state_t_scr[...] = jnp.zeros_like(state_t_scr)

        state_prev_scr[...] = state_t_scr[...]
        fast = functools.partial(
            _recurrence_fast, states_t=[[state_t_scr[0, hd] for hd in range(H_A)]],
            group=PROMPT_GROUP, chunk=PROMPT_CHUNK, carry_state=True)
        y, _, (new_states, low) = _layer_block(xp_ref[0], refs, fast, group=PROMPT_GROUP, mix_chunk=MLP_CHUNK)
        yp_ref[0] = y
        for hd in range(H_A):
            state_t_scr[0, hd] = new_states[-1][hd]

        @pl.when(too_steep(low))
        def _():
            state_t_scr[...] = state_prev_scr[...]
            exact = functools.partial(_recurrence_exact, work_ref=work_scr, state_ref=state_t_scr,
                                      group=PROMPT_GROUP, chunk=PROMPT_BLOCK)
            y_exact, _, _ = _layer_block(xp_ref[0], refs, exact, group=PROMPT_GROUP, mix_chunk=MLP_CHUNK)
            yp_ref[0] = y_exact

        @pl.when(j == steps_per_stream - 1)
        def _():
            for hd in range(H_A):
                state_p_ref[0, hd] = state_t_scr[0, hd].T

    @pl.when(step == prompt_steps)
    def _sample():
        rows = n_streams * frames
        fast = functools.partial(
            _recurrence_fast,
            states_t=[[state_in_ref[s, hd].T for hd in range(H_A)] for s in range(n_streams)],
            group=rows, chunk=frames, carry_state=False)
        y, v_norm, (new_states, low) = _layer_block(xs_ref[...], refs, fast, group=rows, mix_chunk=frames)
        ys_ref[...] = y
        v_ref[...] = v_norm
        for s in range(n_streams):
            for hd in range(H_A):
                state_s_ref[s, hd] = new_states[s][hd].T

        @pl.when(too_steep(low))
        def _():
            for s in range(n_streams):
                for hd in range(H_A):
                    state_s_ref[s, hd] = state_in_ref[s, hd].T
            exact = functools.partial(_recurrence_exact, work_ref=work_scr, state_ref=state_s_ref,
                                      group=rows, chunk=frames)
            y_exact, v_exact, _ = _layer_block(xs_ref[...], refs, exact, group=rows, mix_chunk=frames)
            ys_ref[...] = y_exact
            v_ref[...] = v_exact
            for s in range(n_streams):
                for hd in range(H_A):
                    state_s_ref[s, hd] = state_s_ref[s, hd].T


def _full(shape):
    return pl.BlockSpec(shape, lambda *_: (0,) * len(shape))


def kernel(x_prompt, x_sample, state_hgrn, norm_w, w_in, lb_logits, g_norm_w, ln_v_w, ln_v_b, w_s, b_s, w_out, final_norm_w):
    batch, seq, _ = x_prompt.shape
    dec_batch, dec_seq, _ = x_sample.shape
    assert norm_w.shape[0] == 1, "single-layer trunk"
    assert seq % PROMPT_BLOCK == 0 and PROMPT_BLOCK % PROMPT_GROUP == 0
    assert PROMPT_GROUP % MLP_CHUNK == 0 and MLP_CHUNK % PROMPT_CHUNK == 0
    assert dec_seq <= HGRN_CHUNK and dec_seq <= MLP_CHUNK and dec_seq & (dec_seq - 1) == 0
    assert w_s.shape[2:] == (MLP_CHUNK, MLP_CHUNK)

    rp = PROMPT_BLOCK
    rs = dec_batch * dec_seq
    steps_per_stream = seq // rp
    prompt_steps = batch * steps_per_stream
    row = lambda a: a.reshape(1, -1)

    def prompt_block(step):
        blk = jnp.minimum(step, prompt_steps - 1)
        return blk // steps_per_stream, blk % steps_per_stream

    def x_map(step):
        b, j = prompt_block(step)
        return b, j, 0

    def state_map(step):
        b, _ = prompt_block(step)
        return b, 0, 0, 0

    y_prompt, state_p, y_sample, state_s, v_sample = pl.pallas_call(
        functools.partial(_fused_kernel, prompt_steps=prompt_steps, steps_per_stream=steps_per_stream,
                          n_streams=dec_batch, frames=dec_seq),
        grid=(prompt_steps + 1,),
        in_specs=[pl.BlockSpec((1, rp, D_MODEL), x_map),
                  _full((rs, D_MODEL)), _full((dec_batch, H_A, DK, DV)),
                  _full((1, D_MODEL)), pl.BlockSpec(memory_space=pl.ANY), _full(lb_logits.shape),
                  _full((1, D_A)), _full((1, D_B)), _full((1, D_B)),
                  _full((H_B, MLP_CHUNK, MLP_CHUNK)), _full((H_B, MLP_CHUNK)),
                  pl.BlockSpec(memory_space=pl.ANY), _full((1, D_MODEL))],
        out_specs=[pl.BlockSpec((1, rp, D_MODEL), x_map),
                   pl.BlockSpec((1, H_A, DK, DV), state_map),
                   _full((rs, D_MODEL)), _full((dec_batch, H_A, DK, DV)), _full((rs, D_B))],
        out_shape=[jax.ShapeDtypeStruct((batch, seq, D_MODEL), _F32),
                   jax.ShapeDtypeStruct((batch, H_A, DK, DV), _F32),
                   jax.ShapeDtypeStruct((rs, D_MODEL), _F32),
                   jax.ShapeDtypeStruct((dec_batch, H_A, DK, DV), _F32),
                   jax.ShapeDtypeStruct((rs, D_B), _F32)],
        scratch_shapes=[pltpu.VMEM((1, H_A, DV, DK), _F32),
                        pltpu.VMEM((1, H_A, DV, DK), _F32),
                        pltpu.VMEM((5, max(rp, rs), D_A), _F32),
                        pltpu.VMEM((D_MODEL, IN_WIDTH), _BF16),
                        pltpu.VMEM((D_MODEL, D_MODEL), _BF16),
                        pltpu.VMEM((2, D_MODEL, WEIGHT_STAGE_COLS), _F32),
                        pltpu.SemaphoreType.DMA((2,))],
        compiler_params=pltpu.CompilerParams(
            dimension_semantics=("arbitrary",), vmem_limit_bytes=VMEM_LIMIT_BYTES),
        name="layer_step",
    )(x_prompt, x_sample.reshape(rs, D_MODEL), state_hgrn[0],
      row(norm_w[0]), w_in.reshape(D_MODEL, IN_WIDTH), lb_logits, row(g_norm_w[0]), row(ln_v_w[0]), row(ln_v_b[0]),
      w_s[0], b_s[0], w_out.reshape(D_MODEL, D_MODEL), row(final_norm_w))

    return (y_prompt,
            y_sample.reshape(dec_batch, dec_seq, D_MODEL),
            state_p[None],
            state_s[None],
            v_sample.reshape(1, dec_batch, dec_seq, D_B))
```

```python
import functools
import math

import jax
import jax.numpy as jnp
from jax import lax
from jax.experimental import pallas as pl
from jax.experimental.pallas import tpu as pltpu

D_MODEL = 1024
D_A = 512
D_B = 512
H_A = 4
DK = 128
DV = 128
H_B = 4
C_B = 128
MLP_CHUNK = 128
HGRN_CHUNK = 64
IN_WIDTH = 4 * D_A + 3 * D_B
EPS = 1e-6

PROMPT_BLOCK = 512
PROMPT_GROUP = 128
PROMPT_CHUNK = 128
SUBLANES = 8
LOG_DECAY_LIMIT = 78.0
VMEM_LIMIT_BYTES = 56 * 1024 * 1024
PROJECTION_PLAN = {0: (1, 0), 1: (2, 4), 3: (5,), 5: (3, 6)}

WEIGHT_STAGE_COLS = 512
OUT_PARTS = 4

_NT = (((1,), (1,)), ((), ()))
_TN = (((0,), (0,)), ((), ()))
_F32 = jnp.float32
_BF16 = jnp.bfloat16


def _dot(a, b, dims=None):
    if dims is None:
        return jnp.dot(a, b, preferred_element_type=_F32)
    return lax.dot_general(a, b, dims, preferred_element_type=_F32)


def _rms(x, w):
    return x * lax.rsqrt(jnp.mean(x * x, axis=-1, keepdims=True) + EPS) * w


def _half_tanh_half(x):
    h = 0.5 * x
    return jnp.tanh(h), h


def _silu(x):
    t, h = _half_tanh_half(x)
    return h * t + h


def _gelu_exact(x):
    return 0.5 * x * (1.0 + lax.erf(x * math.sqrt(0.5)))


def _same_chunk_causal(rows, chunk):
    shift = chunk.bit_length() - 1
    assert chunk == 1 << shift
    t = lax.broadcasted_iota(jnp.int32, (rows, rows), 0)
    s = lax.broadcasted_iota(jnp.int32, (rows, rows), 1)
    same = lax.shift_right_logical(t, shift) == lax.shift_right_logical(s, shift)
    return jnp.logical_and(same, s <= t)


def _lane_cat(a, group):
    n = a.shape[0] // group
    if n == 1:
        return a
    return jnp.concatenate([a[gi * group:(gi + 1) * group] for gi in range(n)], axis=1)


def _lane_split(a, n):
    if n == 1:
        return a
    w = a.shape[1] // n
    return jnp.concatenate([a[:, gi * w:(gi + 1) * w] for gi in range(n)], axis=0)


def _chunk_cumsum(g, chunk):
    rows, width = g.shape
    n_regs = rows // SUBLANES
    x = g.reshape(n_regs, SUBLANES, width)
    sub = lax.broadcasted_iota(jnp.int32, x.shape, 1)
    shift = 1
    while shift < SUBLANES:
        x = x + jnp.where(sub >= shift, pltpu.roll(x, shift, 1), 0.0)
        shift *= 2
    regs_per_chunk = chunk // SUBLANES
    if regs_per_chunk > 1:
        parts, run = [], None
        for r in range(n_regs):
            part = x[r] + run if r % regs_per_chunk else x[r]
            run = part[SUBLANES - 1:SUBLANES, :]
            parts.append(part)
        return jnp.concatenate(parts, axis=0)
    return x.reshape(rows, width)


def _recurrence_fast(qs, kk, g, iv, side, *, states_t, group, chunk, carry_state):
    rows = qs.shape[0]
    n_groups = rows // group
    chunks_per_group = group // chunk
    n_chunks = rows // chunk
    blk_mask = _same_chunk_causal(group, chunk)
    b = _chunk_cumsum(g, chunk)
    side(1)

    iv = iv().astype(_BF16)
    q_dec = (qs * jnp.exp(b)).astype(_BF16)
    k_inv = kk * jnp.exp(-b)
    k_inv_bf = k_inv.astype(_BF16)

    low = None
    for c in range(n_chunks):
        b_last = b[(c + 1) * chunk - 1:(c + 1) * chunk, :]
        low = b_last if low is None else jnp.minimum(low, b_last)

    new_states = [[None] * H_A for _ in range(n_chunks)]
    o_heads = []
    for hd in range(H_A):
        ln = slice(hd * DK, (hd + 1) * DK)
        groups = [slice(gi * group, (gi + 1) * group) for gi in range(n_groups)]
        chunks = [slice(c * chunk, (c + 1) * chunk) for c in range(n_chunks)]
        scores = [jnp.where(blk_mask, _dot(q_dec[rg, ln], k_inv_bf[rg, ln], _NT), 0.0).astype(_BF16)
                  for rg in groups]
        dec_last, grow = [], []
        for c, rw in enumerate(chunks):
            last = (c + 1) * chunk - 1
            dec_last.append(jnp.exp(b[last:last + 1, ln]))
            k_upd = (k_inv[rw, ln] * dec_last[c]).astype(_BF16)
            grow.append(_dot(iv[rw, ln], k_upd, _TN))
        intra = [_dot(sc, iv[rg, ln]) for sc, rg in zip(scores, groups)]
        o_chunks = []
        s_t = states_t[0][hd]
        for c, rw in enumerate(chunks):
            if not carry_state:
                s_t = states_t[c][hd]
            inter = _dot(q_dec[rw, ln], s_t.astype(_BF16), _NT)
            gi, ci = divmod(c, chunks_per_group)
            o_chunks.append(inter + intra[gi][ci * chunk:(ci + 1) * chunk])
            s_t = s_t * dec_last[c] + grow[c]
            new_states[c][hd] = s_t
        o_heads.append(jnp.concatenate(o_chunks, axis=0) if n_chunks > 1 else o_chunks[0])
        side(2 + hd)
    return jnp.concatenate(o_heads, axis=1), (new_states, jnp.min(low))


def _recurrence_exact(qs, kk, g, iv, side, *, work_ref, state_ref, group, chunk):
    rows = qs.shape[0]
    assert chunk % SUBLANES == 0 and group % SUBLANES == 0
    minis_per_chunk = chunk // SUBLANES
    b = _chunk_cumsum(g, SUBLANES)
    for k, a in enumerate((qs, kk, b, iv())):
        work_ref[k, 0:rows, :] = a
    t_idx = lax.broadcasted_iota(jnp.int32, (SUBLANES, DK), 0)

    def mini(m, carry):
        r0 = pl.multiple_of(m * SUBLANES, SUBLANES)
        qs8, kk8, b8, iv8 = (work_ref[k, pl.ds(r0, SUBLANES), :] for k in range(4))
        c = m // minis_per_chunk
        for hd in range(H_A):
            ln = slice(hd * DK, (hd + 1) * DK)
            q_h, k_h, b_h, v_h = qs8[:, ln], kk8[:, ln], b8[:, ln], iv8[:, ln]
            s_t = state_ref[c, hd]
            acc = _dot(q_h * jnp.exp(b_h), s_t, _NT)
            for s in range(SUBLANES):
                dec = jnp.exp(jnp.minimum(b_h - b_h[s:s + 1, :], 0.0))
                term = jnp.where(t_idx >= s, q_h * k_h[s:s + 1, :] * dec, 0.0)
                acc = acc + jnp.sum(term, axis=1, keepdims=True) * v_h[s:s + 1, :]
            work_ref[4, pl.ds(r0, SUBLANES), ln] = acc
            b_end = b_h[SUBLANES - 1:SUBLANES, :]
            k_upd = k_h * jnp.exp(b_end - b_h)
            state_ref[c, hd] = s_t * jnp.exp(b_end) + _dot(v_h, k_upd, _TN)
        return carry

    lax.fori_loop(0, rows // SUBLANES, mini, 0)
    return work_ref[4, 0:rows, :], None


def _layer_block(x, refs, recurrence, *, group, mix_chunk):
    (norm_w_ref, w_in_ref, lb_logits_ref, g_norm_w_ref, ln_v_w_ref, ln_v_b_ref,
     w_s_ref, b_s_ref, w_out_ref, final_w_ref) = refs
    n_groups = x.shape[0] // group

    h = _rms(x, norm_w_ref[...]).astype(_BF16)
    slices = {}

    def proj(k):
        if k not in slices:
            slices[k] = _dot(h, w_in_ref[:, k * D_A:(k + 1) * D_A])
        return slices[k]

    def side(point):
        for k in PROJECTION_PLAN.get(point, ()):
            proj(k)

    logits = lb_logits_ref[...]
    e = jnp.exp(logits - jnp.max(logits, axis=0, keepdims=True))
    lb = e[0:1, :] / jnp.sum(e, axis=0, keepdims=True)

    side(0)
    half_span = 0.5 * (1.0 - lb)
    f = (lb + half_span) + half_span * _half_tanh_half(proj(1))[0]
    qs = _silu(proj(0)) * (DK ** -0.5)
    o, aux = recurrence(qs, 1.0 - f, jnp.log(f), lambda: proj(2), side)

    ga = proj(3)
    y_a_heads = []
    for hd in range(H_A):
        o_h = o[:, hd * DV:(hd + 1) * DV]
        y_a_heads.append(o_h * lax.rsqrt(jnp.mean(o_h * o_h, axis=-1, keepdims=True) + EPS))
    y_a = jnp.concatenate(y_a_heads, axis=1) * g_norm_w_ref[...] * _silu(ga)

    u = _gelu_exact(proj(4))
    v = _gelu_exact(proj(5))
    mu = jnp.mean(v, axis=-1, keepdims=True)
    vc = v - mu
    var = jnp.mean(vc * vc, axis=-1, keepdims=True)
    v_norm = vc * lax.rsqrt(var + EPS) * ln_v_w_ref[...] + ln_v_b_ref[...]
    gb = proj(6)
    v_bf = v_norm.astype(_BF16)
    mix_mask = _same_chunk_causal(group, mix_chunk)
    pos_r = lax.broadcasted_iota(jnp.int32, (group, MLP_CHUNK), 0)
    pos_c = lax.broadcasted_iota(jnp.int32, (group, MLP_CHUNK), 1)
    pos_mask = jnp.bitwise_and(pos_r, mix_chunk - 1) == pos_c
    mix_heads = []
    for hb in range(H_B):
        ln = slice(hb * C_B, (hb + 1) * C_B)
        if group == mix_chunk == MLP_CHUNK:
            w_full = w_s_ref[hb]
        else:
            w_rows = jnp.concatenate([w_s_ref[hb][:mix_chunk]] * (group // mix_chunk), axis=0)
            w_full = _dot(w_rows.astype(_BF16), jnp.where(pos_mask, 1.0, 0.0).astype(_BF16), _NT)
        w = jnp.where(mix_mask, w_full, 0.0).astype(_BF16)
        bias = jnp.sum(jnp.where(pos_mask, b_s_ref[hb:hb + 1, :], 0.0), axis=1, keepdims=True)
        mixed = _dot(w, _lane_cat(v_bf[:, ln], group)) + bias
        mix_heads.append(_lane_split(mixed, n_groups))
    mix = jnp.concatenate(mix_heads, axis=1)
    y_b = u * mix * _silu(gb)

    y = jnp.concatenate([y_a, y_b], axis=1).astype(_BF16)
    part = max(x.shape[0] // OUT_PARTS, group)
    outs = []
    for r0 in range(0, x.shape[0], part):
        out = x[r0:r0 + part] + _dot(y[r0:r0 + part], w_out_ref[...])
        outs.append(_rms(out, final_w_ref[...]))
    return jnp.concatenate(outs, axis=0), v_norm, aux


def _stage_weights_bf16(pairs, stage_ref, sem_ref):
    chunks = [(src, dst, c0) for src, dst in pairs for c0 in range(0, src.shape[1], WEIGHT_STAGE_COLS)]

    def copy(i):
        src, _, c0 = chunks[i]
        return pltpu.make_async_copy(src.at[:, c0:c0 + WEIGHT_STAGE_COLS], stage_ref.at[i % 2], sem_ref.at[i % 2])

    copy(0).start()
    for i, (_, dst, c0) in enumerate(chunks):
        if i + 1 < len(chunks):
            copy(i + 1).start()
        copy(i).wait()
        dst[:, c0:c0 + WEIGHT_STAGE_COLS] = stage_ref[i % 2].astype(_BF16)


def _fused_kernel(xp_ref, xs_ref, state_in_ref, norm_w_ref, w_in_hbm, lb_logits_ref, g_norm_w_ref,
                  ln_v_w_ref, ln_v_b_ref, w_s_ref, b_s_ref, w_out_hbm, final_w_ref,
                  yp_ref, state_p_ref, ys_ref, state_s_ref, v_ref,
                  state_t_scr, state_prev_scr, work_scr, w_in_ref, w_out_ref, stage_scr, stage_sem,
                  *, prompt_steps, steps_per_stream, n_streams, frames):
    step = pl.program_id(0)
    refs = (norm_w_ref, w_in_ref, lb_logits_ref, g_norm_w_ref, ln_v_w_ref, ln_v_b_ref,
            w_s_ref, b_s_ref, w_out_ref, final_w_ref)

    @pl.when(step == 0)
    def _():
        _stage_weights_bf16([(w_in_hbm, w_in_ref), (w_out_hbm, w_out_ref)], stage_scr, stage_sem)

    def too_steep(low):
        return jnp.logical_not(low >= -LOG_DECAY_LIMIT)

    @pl.when(step < prompt_steps)
    def _prompt():
        j = lax.rem(step, steps_per_stream)

        @pl.when(j == 0)
        def _():
            state_t_scr[...] = jnp.zeros_like(state_t_scr)

        state_prev_scr[...] = state_t_scr[...]
        fast = functools.partial(
            _recurrence_fast, states_t=[[state_t_scr[0, hd] for hd in range(H_A)]],
            group=PROMPT_GROUP, chunk=PROMPT_CHUNK, carry_state=True)
        y, _, (new_states, low) = _layer_block(xp_ref[0], refs, fast, group=PROMPT_GROUP, mix_chunk=MLP_CHUNK)
        yp_ref[0] = y
        for hd in range(H_A):
            state_t_scr[0, hd] = new_states[-1][hd]

        @pl.when(too_steep(low))
        def _():
            state_t_scr[...] = state_prev_scr[...]
            exact = functools.partial(_recurrence_exact, work_ref=work_scr, state_ref=state_t_scr,
                                      group=PROMPT_GROUP, chunk=PROMPT_BLOCK)
            y_exact, _, _ = _layer_block(xp_ref[0], refs, exact, group=PROMPT_GROUP, mix_chunk=MLP_CHUNK)
            yp_ref[0] = y_exact

        @pl.when(j == steps_per_stream - 1)
        def _():
            for hd in range(H_A):
                state_p_ref[0, hd] = state_t_scr[0, hd].T

    @pl.when(step == prompt_steps)
    def _sample():
        rows = n_streams * frames
        fast = functools.partial(
            _recurrence_fast,
            states_t=[[state_in_ref[s, hd].T for hd in range(H_A)] for s in range(n_streams)],
            group=rows, chunk=frames, carry_state=False)
        y, v_norm, (new_states, low) = _layer_block(xs_ref[...], refs, fast, group=rows, mix_chunk=frames)
        ys_ref[...] = y
        v_ref[...] = v_norm
        for s in range(n_streams):
            for hd in range(H_A):
                state_s_ref[s, hd] = new_states[s][hd].T

        @pl.when(too_steep(low))
        def _():
            for s in range(n_streams):
                for hd in range(H_A):
                    state_s_ref[s, hd] = state_in_ref[s, hd].T
            exact = functools.partial(_recurrence_exact, work_ref=work_scr, state_ref=state_s_ref,
                                      group=rows, chunk=frames)
            y_exact, v_exact, _ = _layer_block(xs_ref[...], refs, exact, group=rows, mix_chunk=frames)
            ys_ref[...] = y_exact
            v_ref[...] = v_exact
            for s in range(n_streams):
                for hd in range(H_A):
                    state_s_ref[s, hd] = state_s_ref[s, hd].T


def _full(shape):
    return pl.BlockSpec(shape, lambda *_: (0,) * len(shape))


def kernel(x_prompt, x_sample, state_hgrn, norm_w, w_in, lb_logits, g_norm_w, ln_v_w, ln_v_b, w_s, b_s, w_out, final_norm_w):
    batch, seq, _ = x_prompt.shape
    dec_batch, dec_seq, _ = x_sample.shape
    assert norm_w.shape[0] == 1, "single-layer trunk"
    assert seq % PROMPT_BLOCK == 0 and PROMPT_BLOCK % PROMPT_GROUP == 0
    assert PROMPT_GROUP % MLP_CHUNK == 0 and MLP_CHUNK % PROMPT_CHUNK == 0
    assert dec_seq <= HGRN_CHUNK and dec_seq <= MLP_CHUNK and dec_seq & (dec_seq - 1) == 0
    assert w_s.shape[2:] == (MLP_CHUNK, MLP_CHUNK)

    rp = PROMPT_BLOCK
    rs = dec_batch * dec_seq
    steps_per_stream = seq // rp
    prompt_steps = batch * steps_per_stream
    row = lambda a: a.reshape(1, -1)

    def prompt_block(step):
        blk = jnp.minimum(step, prompt_steps - 1)
        return blk // steps_per_stream, blk % steps_per_stream

    def x_map(step):
        b, j = prompt_block(step)
        return b, j, 0

    def state_map(step):
        b, _ = prompt_block(step)
        return b, 0, 0, 0

    y_prompt, state_p, y_sample, state_s, v_sample = pl.pallas_call(
        functools.partial(_fused_kernel, prompt_steps=prompt_steps, steps_per_stream=steps_per_stream,
                          n_streams=dec_batch, frames=dec_seq),
        grid=(prompt_steps + 1,),
        in_specs=[pl.BlockSpec((1, rp, D_MODEL), x_map),
                  _full((rs, D_MODEL)), _full((dec_batch, H_A, DK, DV)),
                  _full((1, D_MODEL)), pl.BlockSpec(memory_space=pl.ANY), _full(lb_logits.shape),
                  _full((1, D_A)), _full((1, D_B)), _full((1, D_B)),
                  _full((H_B, MLP_CHUNK, MLP_CHUNK)), _full((H_B, MLP_CHUNK)),
                  pl.BlockSpec(memory_space=pl.ANY), _full((1, D_MODEL))],
        out_specs=[pl.BlockSpec((1, rp, D_MODEL), x_map),
                   pl.BlockSpec((1, H_A, DK, DV), state_map),
                   _full((rs, D_MODEL)), _full((dec_batch, H_A, DK, DV)), _full((rs, D_B))],
        out_shape=[jax.ShapeDtypeStruct((batch, seq, D_MODEL), _F32),
                   jax.ShapeDtypeStruct((batch, H_A, DK, DV), _F32),
                   jax.ShapeDtypeStruct((rs, D_MODEL), _F32),
                   jax.ShapeDtypeStruct((dec_batch, H_A, DK, DV), _F32),
                   jax.ShapeDtypeStruct((rs, D_B), _F32)],
        scratch_shapes=[pltpu.VMEM((1, H_A, DV, DK), _F32),
                        pltpu.VMEM((1, H_A, DV, DK), _F32),
                        pltpu.VMEM((5, max(rp, rs), D_A), _F32),
                        pltpu.VMEM((D_MODEL, IN_WIDTH), _BF16),
                        pltpu.VMEM((D_MODEL, D_MODEL), _BF16),
                        pltpu.VMEM((2, D_MODEL, WEIGHT_STAGE_COLS), _F32),
                        pltpu.SemaphoreType.DMA((2,))],
        compiler_params=pltpu.CompilerParams(
            dimension_semantics=("arbitrary",), vmem_limit_bytes=VMEM_LIMIT_BYTES),
        name="layer_step",
    )(x_prompt, x_sample.reshape(rs, D_MODEL), state_hgrn[0],
      row(norm_w[0]), w_in.reshape(D_MODEL, IN_WIDTH), lb_logits, row(g_norm_w[0]), row(ln_v_w[0]), row(ln_v_b[0]),
      w_s[0], b_s[0], w_out.reshape(D_MODEL, D_MODEL), row(final_norm_w))

    return (y_prompt,
            y_sample.reshape(dec_batch, dec_seq, D_MODEL),
            state_p[None],
            state_s[None],
            v_sample.reshape(1, dec_batch, dec_seq, D_B))
```

```python
import functools
import math

import jax
import jax.numpy as jnp
from jax import lax
from jax.experimental import pallas as pl
from jax.experimental.pallas import tpu as pltpu

D_MODEL = 1024
D_A = 512
D_B = 512
H_A = 4
DK = 128
DV = 128
H_B = 4
C_B = 128
MLP_CHUNK = 128
HGRN_CHUNK = 64
IN_WIDTH = 4 * D_A + 3 * D_B
EPS = 1e-6

PROMPT_BLOCK = 512
PROMPT_GROUP = 128
PROMPT_CHUNK = 128
SUBLANES = 8
LOG_DECAY_LIMIT = 78.0
VMEM_LIMIT_BYTES = 56 * 1024 * 1024
PROJECTION_PLAN = {0: (1, 0), 1: (2, 4), 3: (5,), 5: (3, 6)}

WEIGHT_STAGE_COLS = 512
OUT_PARTS = 4
MXU_WIDTH = 256

_NT = (((1,), (1,)), ((), ()))
_TN = (((0,), (0,)), ((), ()))
_F32 = jnp.float32
_BF16 = jnp.bfloat16


def _dot(a, b, dims=None):
    if dims is None:
        return jnp.dot(a, b, preferred_element_type=_F32)
    return lax.dot_general(a, b, dims, preferred_element_type=_F32)


def _rms(x, w):
    return x * lax.rsqrt(jnp.mean(x * x, axis=-1, keepdims=True) + EPS) * w


def _half_tanh_half(x):
    h = 0.5 * x
    return jnp.tanh(h), h


def _silu(x):
    t, h = _half_tanh_half(x)
    return h * t + h


def _gelu_exact(x):
    return 0.5 * x * (1.0 + lax.erf(x * math.sqrt(0.5)))


def _same_chunk_causal(rows, chunk):
    shift = chunk.bit_length() - 1
    assert chunk == 1 << shift
    t = lax.broadcasted_iota(jnp.int32, (rows, rows), 0)
    s = lax.broadcasted_iota(jnp.int32, (rows, rows), 1)
    same = lax.shift_right_logical(t, shift) == lax.shift_right_logical(s, shift)
    return jnp.logical_and(same, s <= t)


def _lane_cat(a, group):
    n = a.shape[0] // group
    if n == 1:
        return a
    return jnp.concatenate([a[gi * group:(gi + 1) * group] for gi in range(n)], axis=1)


def _lane_split(a, n):
    if n == 1:
        return a
    w = a.shape[1] // n
    return jnp.concatenate([a[:, gi * w:(gi + 1) * w] for gi in range(n)], axis=0)


def _block_diag(blocks):
    if len(blocks) == 1:
        return blocks[0]
    zero = jnp.zeros_like(blocks[0])
    return jnp.concatenate(
        [jnp.concatenate([blk if j == i else zero for j in range(len(blocks))], axis=1)
         for i, blk in enumerate(blocks)], axis=0)


def _chunk_cumsum(g, chunk):
    rows, width = g.shape
    n_regs = rows // SUBLANES
    x = g.reshape(n_regs, SUBLANES, width)
    sub = lax.broadcasted_iota(jnp.int32, x.shape, 1)
    shift = 1
    while shift < SUBLANES:
        x = x + jnp.where(sub >= shift, pltpu.roll(x, shift, 1), 0.0)
        shift *= 2
    regs_per_chunk = chunk // SUBLANES
    if regs_per_chunk > 1:
        parts, run = [], None
        for r in range(n_regs):
            part = x[r] + run if r % regs_per_chunk else x[r]
            run = part[SUBLANES - 1:SUBLANES, :]
            parts.append(part)
        return jnp.concatenate(parts, axis=0)
    return x.reshape(rows, width)


def _recurrence_fast(qs, kk, g, iv, side, *, states_t, group, chunk, carry_state):
    rows = qs.shape[0]
    n_groups = rows // group
    chunks_per_group = group // chunk
    n_chunks = rows // chunk
    blk_mask = _same_chunk_causal(group, chunk)
    b = _chunk_cumsum(g, chunk)
    side(1)

    iv = iv().astype(_BF16)
    q_dec = (qs * jnp.exp(b)).astype(_BF16)
    k_inv = kk * jnp.exp(-b)
    k_inv_bf = k_inv.astype(_BF16)

    low = None
    for c in range(n_chunks):
        b_last = b[(c + 1) * chunk - 1:(c + 1) * chunk, :]
        low = b_last if low is None else jnp.minimum(low, b_last)

    new_states = [[None] * H_A for _ in range(n_chunks)]
    groups = [slice(gi * group, (gi + 1) * group) for gi in range(n_groups)]
    chunks = [slice(c * chunk, (c + 1) * chunk) for c in range(n_chunks)]
    pack = MXU_WIDTH // DK if chunks_per_group == 1 else 1
    pack_mask = jnp.concatenate([blk_mask] * pack, axis=1) if pack > 1 else blk_mask
    o_packs = []
    for h0 in range(0, H_A, pack):
        heads = range(h0, h0 + pack)
        lanes = [slice(hd * DK, (hd + 1) * DK) for hd in heads]
        ln = slice(h0 * DK, (h0 + pack) * DK)
        scores = [jnp.where(pack_mask,
                            _dot(q_dec[rg, ln], _block_diag([k_inv_bf[rg, l] for l in lanes]), _NT),
                            0.0).astype(_BF16) for rg in groups]
        dec_last = [[None] * n_chunks for _ in heads]
        grow = [[None] * n_chunks for _ in heads]
        for i, l in enumerate(lanes):
            for c, rw in enumerate(chunks):
                last = (c + 1) * chunk - 1
                dec_last[i][c] = jnp.exp(b[last:last + 1, l])
                k_upd = (k_inv[rw, l] * dec_last[i][c]).astype(_BF16)
                grow[i][c] = _dot(iv[rw, l], k_upd, _TN)
        intra = [_dot(sc, _block_diag([iv[rg, l] for l in lanes]))
                 for sc, rg in zip(scores, groups)]
        o_chunks = []
        s_t = [states_t[0][hd] for hd in heads]
        for c, rw in enumerate(chunks):
            if not carry_state:
                s_t = [states_t[c][hd] for hd in heads]
            inter = _dot(q_dec[rw, ln], _block_diag([s.astype(_BF16) for s in s_t]), _NT)
            gi, ci = divmod(c, chunks_per_group)
            o_chunks.append(inter + intra[gi][ci * chunk:(ci + 1) * chunk])
            for i, hd in enumerate(heads):
                s_t[i] = s_t[i] * dec_last[i][c] + grow[i][c]
                new_states[c][hd] = s_t[i]
        o_packs.append(jnp.concatenate(o_chunks, axis=0) if n_chunks > 1 else o_chunks[0])
        for hd in heads:
            side(2 + hd)
    return jnp.concatenate(o_packs, axis=1), (new_states, jnp.min(low))


def _recurrence_exact(qs, kk, g, iv, side, *, work_ref, state_ref, group, chunk):
    rows = qs.shape[0]
    assert chunk % SUBLANES == 0 and group % SUBLANES == 0
    minis_per_chunk = chunk // SUBLANES
    b = _chunk_cumsum(g, SUBLANES)
    for k, a in enumerate((qs, kk, b, iv())):
        work_ref[k, 0:rows, :] = a
    t_idx = lax.broadcasted_iota(jnp.int32, (SUBLANES, DK), 0)

    def mini(m, carry):
        r0 = pl.multiple_of(m * SUBLANES, SUBLANES)
        qs8, kk8, b8, iv8 = (work_ref[k, pl.ds(r0, SUBLANES), :] for k in range(4))
        c = m // minis_per_chunk
        for hd in range(H_A):
            ln = slice(hd * DK, (hd + 1) * DK)
            q_h, k_h, b_h, v_h = qs8[:, ln], kk8[:, ln], b8[:, ln], iv8[:, ln]
            s_t = state_ref[c, hd]
            acc = _dot(q_h * jnp.exp(b_h), s_t, _NT)
            for s in range(SUBLANES):
                dec = jnp.exp(jnp.minimum(b_h - b_h[s:s + 1, :], 0.0))
                term = jnp.where(t_idx >= s, q_h * k_h[s:s + 1, :] * dec, 0.0)
                acc = acc + jnp.sum(term, axis=1, keepdims=True) * v_h[s:s + 1, :]
            work_ref[4, pl.ds(r0, SUBLANES), ln] = acc
            b_end = b_h[SUBLANES - 1:SUBLANES, :]
            k_upd = k_h * jnp.exp(b_end - b_h)
            state_ref[c, hd] = s_t * jnp.exp(b_end) + _dot(v_h, k_upd, _TN)
        return carry

    lax.fori_loop(0, rows // SUBLANES, mini, 0)
    return work_ref[4, 0:rows, :], None


def _layer_block(x, refs, recurrence, *, group, mix_chunk):
    (norm_w_ref, w_in_ref, lb_logits_ref, g_norm_w_ref, ln_v_w_ref, ln_v_b_ref,
     w_s_ref, b_s_ref, w_out_ref, final_w_ref) = refs
    n_groups = x.shape[0] // group

    h = _rms(x, norm_w_ref[...]).astype(_BF16)
    slices = {}

    def proj(k):
        if k not in slices:
            slices[k] = _dot(h, w_in_ref[:, k * D_A:(k + 1) * D_A])
        return slices[k]

    def side(point):
        for k in PROJECTION_PLAN.get(point, ()):
            proj(k)

    logits = lb_logits_ref[...]
    e = jnp.exp(logits - jnp.max(logits, axis=0, keepdims=True))
    lb = e[0:1, :] / jnp.sum(e, axis=0, keepdims=True)

    side(0)
    half_span = 0.5 * (1.0 - lb)
    f = (lb + half_span) + half_span * _half_tanh_half(proj(1))[0]
    qs = _silu(proj(0)) * (DK ** -0.5)
    o, aux = recurrence(qs, 1.0 - f, jnp.log(f), lambda: proj(2), side)

    ga = proj(3)
    y_a_heads = []
    for hd in range(H_A):
        o_h = o[:, hd * DV:(hd + 1) * DV]
        y_a_heads.append(o_h * lax.rsqrt(jnp.mean(o_h * o_h, axis=-1, keepdims=True) + EPS))
    y_a = jnp.concatenate(y_a_heads, axis=1) * g_norm_w_ref[...] * _silu(ga)

    u = _gelu_exact(proj(4))
    v = _gelu_exact(proj(5))
    mu = jnp.mean(v, axis=-1, keepdims=True)
    vc = v - mu
    var = jnp.mean(vc * vc, axis=-1, keepdims=True)
    v_norm = vc * lax.rsqrt(var + EPS) * ln_v_w_ref[...] + ln_v_b_ref[...]
    gb = proj(6)
    v_bf = v_norm.astype(_BF16)
    mix_mask = _same_chunk_causal(group, mix_chunk)
    pos_r = lax.broadcasted_iota(jnp.int32, (group, MLP_CHUNK), 0)
    pos_c = lax.broadcasted_iota(jnp.int32, (group, MLP_CHUNK), 1)
    pos_mask = jnp.bitwise_and(pos_r, mix_chunk - 1) == pos_c
    mix_heads = []
    for hb in range(H_B):
        ln = slice(hb * C_B, (hb + 1) * C_B)
        if group == mix_chunk == MLP_CHUNK:
            w_full = w_s_ref[hb]
        else:
            w_rows = jnp.concatenate([w_s_ref[hb][:mix_chunk]] * (group // mix_chunk), axis=0)
            w_full = _dot(w_rows.astype(_BF16), jnp.where(pos_mask, 1.0, 0.0).astype(_BF16), _NT)
        w = jnp.where(mix_mask, w_full, 0.0).astype(_BF16)
        bias = jnp.sum(jnp.where(pos_mask, b_s_ref[hb:hb + 1, :], 0.0), axis=1, keepdims=True)
        mixed = _dot(w, _lane_cat(v_bf[:, ln], group)) + bias
        mix_heads.append(_lane_split(mixed, n_groups))
    mix = jnp.concatenate(mix_heads, axis=1)
    y_b = u * mix * _silu(gb)

    y = jnp.concatenate([y_a, y_b], axis=1).astype(_BF16)
    part = max(x.shape[0] // OUT_PARTS, group)
    outs = []
    for r0 in range(0, x.shape[0], part):
        out = x[r0:r0 + part] + _dot(y[r0:r0 + part], w_out_ref[...])
        outs.append(_rms(out, final_w_ref[...]))
    return jnp.concatenate(outs, axis=0), v_norm, aux


def _stage_weights_bf16(pairs, stage_ref, sem_ref):
    chunks = [(src, dst, c0) for src, dst in pairs for c0 in range(0, src.shape[1], WEIGHT_STAGE_COLS)]

    def copy(i):
        src, _, c0 = chunks[i]
        return pltpu.make_async_copy(src.at[:, c0:c0 + WEIGHT_STAGE_COLS], stage_ref.at[i % 2], sem_ref.at[i % 2])

    copy(0).start()
    for i, (_, dst, c0) in enumerate(chunks):
        if i + 1 < len(chunks):
            copy(i + 1).start()
        copy(i).wait()
        dst[:, c0:c0 + WEIGHT_STAGE_COLS] = stage_ref[i % 2].astype(_BF16)


def _fused_kernel(xp_ref, xs_ref, state_in_ref, norm_w_ref, w_in_hbm, lb_logits_ref, g_norm_w_ref,
                  ln_v_w_ref, ln_v_b_ref, w_s_ref, b_s_ref, w_out_hbm, final_w_ref,
                  yp_ref, state_p_ref, ys_ref, state_s_ref, v_ref,
                  state_t_scr, state_prev_scr, work_scr, w_in_ref, w_out_ref, stage_scr, stage_sem,
                  *, prompt_steps, steps_per_stream, n_streams, frames):
    step = pl.program_id(0)
    refs = (norm_w_ref, w_in_ref, lb_logits_ref, g_norm_w_ref, ln_v_w_ref, ln_v_b_ref,
            w_s_ref, b_s_ref, w_out_ref, final_w_ref)

    @pl.when(step == 0)
    def _():
        _stage_weights_bf16([(w_in_hbm, w_in_ref), (w_out_hbm, w_out_ref)], stage_scr, stage_sem)

    def too_steep(low):
        return jnp.logical_not(low >= -LOG_DECAY_LIMIT)

    @pl.when(step < prompt_steps)
    def _prompt():
        j = lax.rem(step, steps_per_stream)

        @pl.when(j == 0)
        def _():
            state_t_scr[...] = jnp.zeros_like(state_t_scr)

        state_prev_scr[...] = state_t_scr[...]
        fast = functools.partial(
            _recurrence_fast, states_t=[[state_t_scr[0, hd] for hd in range(H_A)]],
            group=PROMPT_GROUP, chunk=PROMPT_CHUNK, carry_state=True)
        y, _, (new_states, low) = _layer_block(xp_ref[0], refs, fast, group=PROMPT_GROUP, mix_chunk=MLP_CHUNK)
        yp_ref[0] = y
        for hd in range(H_A):
            state_t_scr[0, hd] = new_states[-1][hd]

        @pl.when(too_steep(low))
        def _():
            state_t_scr[...] = state_prev_scr[...]
            exact = functools.partial(_recurrence_exact, work_ref=work_scr, state_ref=state_t_scr,
                                      group=PROMPT_GROUP, chunk=PROMPT_BLOCK)
            y_exact, _, _ = _layer_block(xp_ref[0], refs, exact, group=PROMPT_GROUP, mix_chunk=MLP_CHUNK)
            yp_ref[0] = y_exact

        @pl.when(j == steps_per_stream - 1)
        def _():
            for hd in range(H_A):
                state_p_ref[0, hd] = state_t_scr[0, hd].T

    @pl.when(step == prompt_steps)
    def _sample():
        rows = n_streams * frames
        fast = functools.partial(
            _recurrence_fast,
            states_t=[[state_in_ref[s, hd].T for hd in range(H_A)] for s in range(n_streams)],
            group=rows, chunk=frames, carry_state=False)
        y, v_norm, (new_states, low) = _layer_block(xs_ref[...], refs, fast, group=rows, mix_chunk=frames)
        ys_ref[...] = y
        v_ref[...] = v_norm
        for s in range(n_streams):
            for hd in range(H_A):
                state_s_ref[s, hd] = new_states[s][hd].T

        @pl.when(too_steep(low))
        def _():
            for s in range(n_streams):
                for hd in range(H_A):
                    state_s_ref[s, hd] = state_in_ref[s, hd].T
            exact = functools.partial(_recurrence_exact, work_ref=work_scr, state_ref=state_s_ref,
                                      group=rows, chunk=frames)
            y_exact, v_exact, _ = _layer_block(xs_ref[...], refs, exact, group=rows, mix_chunk=frames)
            ys_ref[...] = y_exact
            v_ref[...] = v_exact
            for s in range(n_streams):
                for hd in range(H_A):
                    state_s_ref[s, hd] = state_s_ref[s, hd].T


def _full(shape):
    return pl.BlockSpec(shape, lambda *_: (0,) * len(shape))


def kernel(x_prompt, x_sample, state_hgrn, norm_w, w_in, lb_logits, g_norm_w, ln_v_w, ln_v_b, w_s, b_s, w_out, final_norm_w):
    batch, seq, _ = x_prompt.shape
    dec_batch, dec_seq, _ = x_sample.shape
    assert norm_w.shape[0] == 1, "single-layer trunk"
    assert seq % PROMPT_BLOCK == 0 and PROMPT_BLOCK % PROMPT_GROUP == 0
    assert PROMPT_GROUP % MLP_CHUNK == 0 and MLP_CHUNK % PROMPT_CHUNK == 0
    assert dec_seq <= HGRN_CHUNK and dec_seq <= MLP_CHUNK and dec_seq & (dec_seq - 1) == 0
    assert w_s.shape[2:] == (MLP_CHUNK, MLP_CHUNK)

    rp = PROMPT_BLOCK
    rs = dec_batch * dec_seq
    steps_per_stream = seq // rp
    prompt_steps = batch * steps_per_stream
    row = lambda a: a.reshape(1, -1)

    def prompt_block(step):
        blk = jnp.minimum(step, prompt_steps - 1)
        return blk // steps_per_stream, blk % steps_per_stream

    def x_map(step):
        b, j = prompt_block(step)
        return b, j, 0

    def state_map(step):
        b, _ = prompt_block(step)
        return b, 0, 0, 0

    y_prompt, state_p, y_sample, state_s, v_sample = pl.pallas_call(
        functools.partial(_fused_kernel, prompt_steps=prompt_steps, steps_per_stream=steps_per_stream,
                          n_streams=dec_batch, frames=dec_seq),
        grid=(prompt_steps + 1,),
        in_specs=[pl.BlockSpec((1, rp, D_MODEL), x_map),
                  _full((rs, D_MODEL)), _full((dec_batch, H_A, DK, DV)),
                  _full((1, D_MODEL)), pl.BlockSpec(memory_space=pl.ANY), _full(lb_logits.shape),
                  _full((1, D_A)), _full((1, D_B)), _full((1, D_B)),
                  _full((H_B, MLP_CHUNK, MLP_CHUNK)), _full((H_B, MLP_CHUNK)),
                  pl.BlockSpec(memory_space=pl.ANY), _full((1, D_MODEL))],
        out_specs=[pl.BlockSpec((1, rp, D_MODEL), x_map),
                   pl.BlockSpec((1, H_A, DK, DV), state_map),
                   _full((rs, D_MODEL)), _full((dec_batch, H_A, DK, DV)), _full((rs, D_B))],
        out_shape=[jax.ShapeDtypeStruct((batch, seq, D_MODEL), _F32),
                   jax.ShapeDtypeStruct((batch, H_A, DK, DV), _F32),
                   jax.ShapeDtypeStruct((rs, D_MODEL), _F32),
                   jax.ShapeDtypeStruct((dec_batch, H_A, DK, DV), _F32),
                   jax.ShapeDtypeStruct((rs, D_B), _F32)],
        scratch_shapes=[pltpu.VMEM((1, H_A, DV, DK), _F32),
                        pltpu.VMEM((1, H_A, DV, DK), _F32),
                        pltpu.VMEM((5, max(rp, rs), D_A), _F32),
                        pltpu.VMEM((D_MODEL, IN_WIDTH), _BF16),
                        pltpu.VMEM((D_MODEL, D_MODEL), _BF16),
                        pltpu.VMEM((2, D_MODEL, WEIGHT_STAGE_COLS), _F32),
                        pltpu.SemaphoreType.DMA((2,))],
        compiler_params=pltpu.CompilerParams(
            dimension_semantics=("arbitrary",), vmem_limit_bytes=VMEM_LIMIT_BYTES),
        name="layer_step",
    )(x_prompt, x_sample.reshape(rs, D_MODEL), state_hgrn[0],
      row(norm_w[0]), w_in.reshape(D_MODEL, IN_WIDTH), lb_logits, row(g_norm_w[0]), row(ln_v_w[0]), row(ln_v_b[0]),
      w_s[0], b_s[0], w_out.reshape(D_MODEL, D_MODEL), row(final_norm_w))

    return (y_prompt,
            y_sample.reshape(dec_batch, dec_seq, D_MODEL),
            state_p[None],
            state_s[None],
            v_sample.reshape(1, dec_batch, dec_seq, D_B))
```

```python
import functools
import math

import jax
import jax.numpy as jnp
from jax import lax
from jax.experimental import pallas as pl
from jax.experimental.pallas import tpu as pltpu

D_MODEL = 1024
D_A = 512
D_B = 512
H_A = 4
DK = 128
DV = 128
H_B = 4
C_B = 128
MLP_CHUNK = 128
HGRN_CHUNK = 64
IN_WIDTH = 4 * D_A + 3 * D_B
EPS = 1e-6

PROMPT_BLOCK = 512
PROMPT_GROUP = 128
PROMPT_CHUNK = 128
SUBLANES = 8
LOG_DECAY_LIMIT = 78.0
VMEM_LIMIT_BYTES = 56 * 1024 * 1024
PROJECTION_PLAN = {0: (1, 0), 1: (2, 4), 3: (5,), 5: (3, 6)}

WEIGHT_STAGE_COLS = 512
WEIGHT_STAGE_SLOTS = 3
OUT_PARTS = 4

_NT = (((1,), (1,)), ((), ()))
_TN = (((0,), (0,)), ((), ()))
_F32 = jnp.float32
_BF16 = jnp.bfloat16


def _dot(a, b, dims=None):
    if dims is None:
        return jnp.dot(a, b, preferred_element_type=_F32)
    return lax.dot_general(a, b, dims, preferred_element_type=_F32)


def _rms(x, w):
    return x * lax.rsqrt(jnp.mean(x * x, axis=-1, keepdims=True) + EPS) * w


def _half_tanh_half(x):
    h = 0.5 * x
    return jnp.tanh(h), h


def _silu(x):
    t, h = _half_tanh_half(x)
    return h * t + h


def _gelu_exact(x):
    return 0.5 * x * (1.0 + lax.erf(x * math.sqrt(0.5)))


def _same_chunk_causal(rows, chunk):
    shift = chunk.bit_length() - 1
    assert chunk == 1 << shift
    t = lax.broadcasted_iota(jnp.int32, (rows, rows), 0)
    s = lax.broadcasted_iota(jnp.int32, (rows, rows), 1)
    same = lax.shift_right_logical(t, shift) == lax.shift_right_logical(s, shift)
    return jnp.logical_and(same, s <= t)


def _lane_cat(a, group):
    n = a.shape[0] // group
    if n == 1:
        return a
    return jnp.concatenate([a[gi * group:(gi + 1) * group] for gi in range(n)], axis=1)


def _lane_split(a, n):
    if n == 1:
        return a
    w = a.shape[1] // n
    return jnp.concatenate([a[:, gi * w:(gi + 1) * w] for gi in range(n)], axis=0)


def _chunk_cumsum(g, chunk):
    rows, width = g.shape
    n_regs = rows // SUBLANES
    x = g.reshape(n_regs, SUBLANES, width)
    sub = lax.broadcasted_iota(jnp.int32, x.shape, 1)
    shift = 1
    while shift < SUBLANES:
        x = x + jnp.where(sub >= shift, pltpu.roll(x, shift, 1), 0.0)
        shift *= 2
    regs_per_chunk = chunk // SUBLANES
    if regs_per_chunk > 1:
        parts, run = [], None
        for r in range(n_regs):
            part = x[r] + run if r % regs_per_chunk else x[r]
            run = part[SUBLANES - 1:SUBLANES, :]
            parts.append(part)
        return jnp.concatenate(parts, axis=0)
    return x.reshape(rows, width)


def _recurrence_fast(qs, kk, g, iv, side, *, states_t, group, chunk, carry_state):
    rows = qs.shape[0]
    n_groups = rows // group
    chunks_per_group = group // chunk
    n_chunks = rows // chunk
    blk_mask = _same_chunk_causal(group, chunk)
    b = _chunk_cumsum(g, chunk)
    side(1)

    iv = iv().astype(_BF16)
    q_dec = (qs * jnp.exp(b)).astype(_BF16)
    k_inv = kk * jnp.exp(-b)
    k_inv_bf = k_inv.astype(_BF16)

    low = None
    for c in range(n_chunks):
        b_last = b[(c + 1) * chunk - 1:(c + 1) * chunk, :]
        low = b_last if low is None else jnp.minimum(low, b_last)

    new_states = [[None] * H_A for _ in range(n_chunks)]
    o_heads = []
    for hd in range(H_A):
        ln = slice(hd * DK, (hd + 1) * DK)
        groups = [slice(gi * group, (gi + 1) * group) for gi in range(n_groups)]
        chunks = [slice(c * chunk, (c + 1) * chunk) for c in range(n_chunks)]
        scores = [jnp.where(blk_mask, _dot(q_dec[rg, ln], k_inv_bf[rg, ln], _NT), 0.0).astype(_BF16)
                  for rg in groups]
        dec_last, grow = [], []
        for c, rw in enumerate(chunks):
            last = (c + 1) * chunk - 1
            dec_last.append(jnp.exp(b[last:last + 1, ln]))
            k_upd = (k_inv[rw, ln] * dec_last[c]).astype(_BF16)
            grow.append(_dot(iv[rw, ln], k_upd, _TN))
        intra = [_dot(sc, iv[rg, ln]) for sc, rg in zip(scores, groups)]
        o_chunks = []
        s_t = states_t[0][hd]
        for c, rw in enumerate(chunks):
            if not carry_state:
                s_t = states_t[c][hd]
            inter = _dot(q_dec[rw, ln], s_t.astype(_BF16), _NT)
            gi, ci = divmod(c, chunks_per_group)
            o_chunks.append(inter + intra[gi][ci * chunk:(ci + 1) * chunk])
            s_t = s_t * dec_last[c] + grow[c]
            new_states[c][hd] = s_t
        o_heads.append(jnp.concatenate(o_chunks, axis=0) if n_chunks > 1 else o_chunks[0])
        side(2 + hd)
    return jnp.concatenate(o_heads, axis=1), (new_states, jnp.min(low))


def _recurrence_exact(qs, kk, g, iv, side, *, work_ref, state_ref, group, chunk):
    rows = qs.shape[0]
    assert chunk % SUBLANES == 0 and group % SUBLANES == 0
    minis_per_chunk = chunk // SUBLANES
    b = _chunk_cumsum(g, SUBLANES)
    for k, a in enumerate((qs, kk, b, iv())):
        work_ref[k, 0:rows, :] = a
    t_idx = lax.broadcasted_iota(jnp.int32, (SUBLANES, DK), 0)

    def mini(m, carry):
        r0 = pl.multiple_of(m * SUBLANES, SUBLANES)
        qs8, kk8, b8, iv8 = (work_ref[k, pl.ds(r0, SUBLANES), :] for k in range(4))
        c = m // minis_per_chunk
        for hd in range(H_A):
            ln = slice(hd * DK, (hd + 1) * DK)
            q_h, k_h, b_h, v_h = qs8[:, ln], kk8[:, ln], b8[:, ln], iv8[:, ln]
            s_t = state_ref[c, hd]
            acc = _dot(q_h * jnp.exp(b_h), s_t, _NT)
            for s in range(SUBLANES):
                dec = jnp.exp(jnp.minimum(b_h - b_h[s:s + 1, :], 0.0))
                term = jnp.where(t_idx >= s, q_h * k_h[s:s + 1, :] * dec, 0.0)
                acc = acc + jnp.sum(term, axis=1, keepdims=True) * v_h[s:s + 1, :]
            work_ref[4, pl.ds(r0, SUBLANES), ln] = acc
            b_end = b_h[SUBLANES - 1:SUBLANES, :]
            k_upd = k_h * jnp.exp(b_end - b_h)
            state_ref[c, hd] = s_t * jnp.exp(b_end) + _dot(v_h, k_upd, _TN)
        return carry

    lax.fori_loop(0, rows // SUBLANES, mini, 0)
    return work_ref[4, 0:rows, :], None


def _layer_block(x, refs, recurrence, *, group, mix_chunk):
    (norm_w_ref, w_in_ref, lb_logits_ref, g_norm_w_ref, ln_v_w_ref, ln_v_b_ref,
     w_s_ref, b_s_ref, w_out_ref, final_w_ref) = refs
    n_groups = x.shape[0] // group

    h = _rms(x, norm_w_ref[...]).astype(_BF16)
    slices = {}

    def proj(k):
        if k not in slices:
            slices[k] = _dot(h, w_in_ref[:, k * D_A:(k + 1) * D_A])
        return slices[k]

    def side(point):
        for k in PROJECTION_PLAN.get(point, ()):
            proj(k)

    logits = lb_logits_ref[...]
    e = jnp.exp(logits - jnp.max(logits, axis=0, keepdims=True))
    lb = e[0:1, :] / jnp.sum(e, axis=0, keepdims=True)

    side(0)
    half_span = 0.5 * (1.0 - lb)
    f = (lb + half_span) + half_span * _half_tanh_half(proj(1))[0]
    qs = _silu(proj(0)) * (DK ** -0.5)
    o, aux = recurrence(qs, 1.0 - f, jnp.log(f), lambda: proj(2), side)

    ga = proj(3)
    y_a_heads = []
    for hd in range(H_A):
        o_h = o[:, hd * DV:(hd + 1) * DV]
        y_a_heads.append(o_h * lax.rsqrt(jnp.mean(o_h * o_h, axis=-1, keepdims=True) + EPS))
    y_a = jnp.concatenate(y_a_heads, axis=1) * g_norm_w_ref[...] * _silu(ga)

    u = _gelu_exact(proj(4))
    v = _gelu_exact(proj(5))
    mu = jnp.mean(v, axis=-1, keepdims=True)
    vc = v - mu
    var = jnp.mean(vc * vc, axis=-1, keepdims=True)
    v_norm = vc * lax.rsqrt(var + EPS) * ln_v_w_ref[...] + ln_v_b_ref[...]
    gb = proj(6)
    v_bf = v_norm.astype(_BF16)
    mix_mask = _same_chunk_causal(group, mix_chunk)
    pos_r = lax.broadcasted_iota(jnp.int32, (group, MLP_CHUNK), 0)
    pos_c = lax.broadcasted_iota(jnp.int32, (group, MLP_CHUNK), 1)
    pos_mask = jnp.bitwise_and(pos_r, mix_chunk - 1) == pos_c
    mix_heads = []
    for hb in range(H_B):
        ln = slice(hb * C_B, (hb + 1) * C_B)
        if group == mix_chunk == MLP_CHUNK:
            w_full = w_s_ref[hb]
        else:
            w_rows = jnp.concatenate([w_s_ref[hb][:mix_chunk]] * (group // mix_chunk), axis=0)
            w_full = _dot(w_rows.astype(_BF16), jnp.where(pos_mask, 1.0, 0.0).astype(_BF16), _NT)
        w = jnp.where(mix_mask, w_full, 0.0).astype(_BF16)
        bias = jnp.sum(jnp.where(pos_mask, b_s_ref[hb:hb + 1, :], 0.0), axis=1, keepdims=True)
        mixed = _dot(w, _lane_cat(v_bf[:, ln], group)) + bias
        mix_heads.append(_lane_split(mixed, n_groups))
    mix = jnp.concatenate(mix_heads, axis=1)
    y_b = u * mix * _silu(gb)

    y = jnp.concatenate([y_a, y_b], axis=1).astype(_BF16)
    part = max(x.shape[0] // OUT_PARTS, group)
    outs = []
    for r0 in range(0, x.shape[0], part):
        out = x[r0:r0 + part] + _dot(y[r0:r0 + part], w_out_ref[...])
        outs.append(_rms(out, final_w_ref[...]))
    return jnp.concatenate(outs, axis=0), v_norm, aux


def _stage_weights_bf16(pairs, stage_ref, sem_ref):
    chunks = [(src, dst, c0) for src, dst in pairs for c0 in range(0, src.shape[1], WEIGHT_STAGE_COLS)]
    slots = stage_ref.shape[0]
    ahead = slots - 1

    def copy(i):
        src, _, c0 = chunks[i]
        return pltpu.make_async_copy(src.at[:, c0:c0 + WEIGHT_STAGE_COLS], stage_ref.at[i % slots], sem_ref.at[i % slots])

    for i in range(min(ahead, len(chunks))):
        copy(i).start()
    for i, (_, dst, c0) in enumerate(chunks):
        if i + ahead < len(chunks):
            copy(i + ahead).start()
        copy(i).wait()
        dst[:, c0:c0 + WEIGHT_STAGE_COLS] = stage_ref[i % slots].astype(_BF16)


def _fused_kernel(xp_ref, xs_ref, state_in_ref, norm_w_ref, w_in_hbm, lb_logits_ref, g_norm_w_ref,
                  ln_v_w_ref, ln_v_b_ref, w_s_ref, b_s_ref, w_out_hbm, final_w_ref,
                  yp_ref, state_p_ref, ys_ref, state_s_ref, v_ref,
                  state_t_scr, state_prev_scr, work_scr, w_in_ref, w_out_ref, stage_scr, stage_sem,
                  *, prompt_steps, steps_per_stream, n_streams, frames):
    step = pl.program_id(0)
    refs = (norm_w_ref, w_in_ref, lb_logits_ref, g_norm_w_ref, ln_v_w_ref, ln_v_b_ref,
            w_s_ref, b_s_ref, w_out_ref, final_w_ref)

    @pl.when(step == 0)
    def _():
        _stage_weights_bf16([(w_in_hbm, w_in_ref), (w_out_hbm, w_out_ref)], stage_scr, stage_sem)

    def too_steep(low):
        return jnp.logical_not(low >= -LOG_DECAY_LIMIT)

    @pl.when(step < prompt_steps)
    def _prompt():
        j = lax.rem(step, steps_per_stream)

        @pl.when(j == 0)
        def _():
            state_t_scr[...] = jnp.zeros_like(state_t_scr)

        state_prev_scr[...] = state_t_scr[...]
        fast = functools.partial(
            _recurrence_fast, states_t=[[state_t_scr[0, hd] for hd in range(H_A)]],
            group=PROMPT_GROUP, chunk=PROMPT_CHUNK, carry_state=True)
        y, _, (new_states, low) = _layer_block(xp_ref[0], refs, fast, group=PROMPT_GROUP, mix_chunk=MLP_CHUNK)
        yp_ref[0] = y
        for hd in range(H_A):
            state_t_scr[0, hd] = new_states[-1][hd]

        @pl.when(too_steep(low))
        def _():
            state_t_scr[...] = state_prev_scr[...]
            exact = functools.partial(_recurrence_exact, work_ref=work_scr, state_ref=state_t_scr,
                                      group=PROMPT_GROUP, chunk=PROMPT_BLOCK)
            y_exact, _, _ = _layer_block(xp_ref[0], refs, exact, group=PROMPT_GROUP, mix_chunk=MLP_CHUNK)
            yp_ref[0] = y_exact

        @pl.when(j == steps_per_stream - 1)
        def _():
            for hd in range(H_A):
                state_p_ref[0, hd] = state_t_scr[0, hd].T

    @pl.when(step == prompt_steps)
    def _sample():
        rows = n_streams * frames
        fast = functools.partial(
            _recurrence_fast,
            states_t=[[state_in_ref[s, hd].T for hd in range(H_A)] for s in range(n_streams)],
            group=rows, chunk=frames, carry_state=False)
        y, v_norm, (new_states, low) = _layer_block(xs_ref[...], refs, fast, group=rows, mix_chunk=frames)
        ys_ref[...] = y
        v_ref[...] = v_norm
        for s in range(n_streams):
            for hd in range(H_A):
                state_s_ref[s, hd] = new_states[s][hd].T

        @pl.when(too_steep(low))
        def _():
            for s in range(n_streams):
                for hd in range(H_A):
                    state_s_ref[s, hd] = state_in_ref[s, hd].T
            exact = functools.partial(_recurrence_exact, work_ref=work_scr, state_ref=state_s_ref,
                                      group=rows, chunk=frames)
            y_exact, v_exact, _ = _layer_block(xs_ref[...], refs, exact, group=rows, mix_chunk=frames)
            ys_ref[...] = y_exact
            v_ref[...] = v_exact
            for s in range(n_streams):
                for hd in range(H_A):
                    state_s_ref[s, hd] = state_s_ref[s, hd].T


def _full(shape):
    return pl.BlockSpec(shape, lambda *_: (0,) * len(shape))


def kernel(x_prompt, x_sample, state_hgrn, norm_w, w_in, lb_logits, g_norm_w, ln_v_w, ln_v_b, w_s, b_s, w_out, final_norm_w):
    batch, seq, _ = x_prompt.shape
    dec_batch, dec_seq, _ = x_sample.shape
    assert norm_w.shape[0] == 1, "single-layer trunk"
    assert seq % PROMPT_BLOCK == 0 and PROMPT_BLOCK % PROMPT_GROUP == 0
    assert PROMPT_GROUP % MLP_CHUNK == 0 and MLP_CHUNK % PROMPT_CHUNK == 0
    assert dec_seq <= HGRN_CHUNK and dec_seq <= MLP_CHUNK and dec_seq & (dec_seq - 1) == 0
    assert w_s.shape[2:] == (MLP_CHUNK, MLP_CHUNK)

    rp = PROMPT_BLOCK
    rs = dec_batch * dec_seq
    steps_per_stream = seq // rp
    prompt_steps = batch * steps_per_stream
    row = lambda a: a.reshape(1, -1)

    def prompt_block(step):
        blk = jnp.minimum(step, prompt_steps - 1)
        return blk // steps_per_stream, blk % steps_per_stream

    def x_map(step):
        b, j = prompt_block(step)
        return b, j, 0

    def state_map(step):
        b, _ = prompt_block(step)
        return b, 0, 0, 0

    y_prompt, state_p, y_sample, state_s, v_sample = pl.pallas_call(
        functools.partial(_fused_kernel, prompt_steps=prompt_steps, steps_per_stream=steps_per_stream,
                          n_streams=dec_batch, frames=dec_seq),
        grid=(prompt_steps + 1,),
        in_specs=[pl.BlockSpec((1, rp, D_MODEL), x_map),
                  _full((rs, D_MODEL)), _full((dec_batch, H_A, DK, DV)),
                  _full((1, D_MODEL)), pl.BlockSpec(memory_space=pl.ANY), _full(lb_logits.shape),
                  _full((1, D_A)), _full((1, D_B)), _full((1, D_B)),
                  _full((H_B, MLP_CHUNK, MLP_CHUNK)), _full((H_B, MLP_CHUNK)),
                  pl.BlockSpec(memory_space=pl.ANY), _full((1, D_MODEL))],
        out_specs=[pl.BlockSpec((1, rp, D_MODEL), x_map),
                   pl.BlockSpec((1, H_A, DK, DV), state_map),
                   _full((rs, D_MODEL)), _full((dec_batch, H_A, DK, DV)), _full((rs, D_B))],
        out_shape=[jax.ShapeDtypeStruct((batch, seq, D_MODEL), _F32),
                   jax.ShapeDtypeStruct((batch, H_A, DK, DV), _F32),
                   jax.ShapeDtypeStruct((rs, D_MODEL), _F32),
                   jax.ShapeDtypeStruct((dec_batch, H_A, DK, DV), _F32),
                   jax.ShapeDtypeStruct((rs, D_B), _F32)],
        scratch_shapes=[pltpu.VMEM((1, H_A, DV, DK), _F32),
                        pltpu.VMEM((1, H_A, DV, DK), _F32),
                        pltpu.VMEM((5, max(rp, rs), D_A), _F32),
                        pltpu.VMEM((D_MODEL, IN_WIDTH), _BF16),
                        pltpu.VMEM((D_MODEL, D_MODEL), _BF16),
                        pltpu.VMEM((WEIGHT_STAGE_SLOTS, D_MODEL, WEIGHT_STAGE_COLS), _F32),
                        pltpu.SemaphoreType.DMA((WEIGHT_STAGE_SLOTS,))],
        compiler_params=pltpu.CompilerParams(
            dimension_semantics=("arbitrary",), vmem_limit_bytes=VMEM_LIMIT_BYTES),
        name="layer_step",
    )(x_prompt, x_sample.reshape(rs, D_MODEL), state_hgrn[0],
      row(norm_w[0]), w_in.reshape(D_MODEL, IN_WIDTH), lb_logits, row(g_norm_w[0]), row(ln_v_w[0]), row(ln_v_b[0]),
      w_s[0], b_s[0], w_out.reshape(D_MODEL, D_MODEL), row(final_norm_w))

    return (y_prompt,
            y_sample.reshape(dec_batch, dec_seq, D_MODEL),
            state_p[None],
            state_s[None],
            v_sample.reshape(1, dec_batch, dec_seq, D_B))
```

```python
import functools
import math

import jax
import jax.numpy as jnp
from jax import lax
from jax.experimental import pallas as pl
from jax.experimental.pallas import tpu as pltpu

D_MODEL = 1024
D_A = 512
D_B = 512
H_A = 4
DK = 128
DV = 128
H_B = 4
C_B = 128
MLP_CHUNK = 128
HGRN_CHUNK = 64
IN_WIDTH = 4 * D_A + 3 * D_B
EPS = 1e-6

PROMPT_BLOCK = 512
PROMPT_GROUP = 128
PROMPT_CHUNK = 128
SUBLANES = 8
LOG_DECAY_LIMIT = 78.0
VMEM_LIMIT_BYTES = 56 * 1024 * 1024
PROJECTION_PLAN = {0: (1, 0), 1: (2, 4), 3: (5,), 5: (3, 6)}

WEIGHT_STAGE_COLS = 512
WEIGHT_STAGE_SLOTS = 3
OUT_PARTS = 4

_NT = (((1,), (1,)), ((), ()))
_TN = (((0,), (0,)), ((), ()))
_F32 = jnp.float32
_BF16 = jnp.bfloat16


def _dot(a, b, dims=None):
    if dims is None:
        return jnp.dot(a, b, preferred_element_type=_F32)
    return lax.dot_general(a, b, dims, preferred_element_type=_F32)


def _rms(x, w):
    return x * lax.rsqrt(jnp.mean(x * x, axis=-1, keepdims=True) + EPS) * w


def _half_tanh_half(x):
    h = 0.5 * x
    return jnp.tanh(h), h


def _silu(x):
    t, h = _half_tanh_half(x)
    return h * t + h


def _gelu_exact(x):
    return 0.5 * x * (1.0 + lax.erf(x * math.sqrt(0.5)))


def _same_chunk_causal(rows, chunk):
    shift = chunk.bit_length() - 1
    assert chunk == 1 << shift
    t = lax.broadcasted_iota(jnp.int32, (rows, rows), 0)
    s = lax.broadcasted_iota(jnp.int32, (rows, rows), 1)
    same = lax.shift_right_logical(t, shift) == lax.shift_right_logical(s, shift)
    return jnp.logical_and(same, s <= t)


def _lane_cat(a, group):
    n = a.shape[0] // group
    if n == 1:
        return a
    return jnp.concatenate([a[gi * group:(gi + 1) * group] for gi in range(n)], axis=1)


def _lane_split(a, n):
    if n == 1:
        return a
    w = a.shape[1] // n
    return jnp.concatenate([a[:, gi * w:(gi + 1) * w] for gi in range(n)], axis=0)


def _chunk_cumsum(g, chunk):
    rows, width = g.shape
    n_regs = rows // SUBLANES
    x = g.reshape(n_regs, SUBLANES, width)
    sub = lax.broadcasted_iota(jnp.int32, x.shape, 1)
    shift = 1
    while shift < SUBLANES:
        x = x + jnp.where(sub >= shift, pltpu.roll(x, shift, 1), 0.0)
        shift *= 2
    regs_per_chunk = chunk // SUBLANES
    if regs_per_chunk > 1:
        parts, run = [], None
        for r in range(n_regs):
            part = x[r] + run if r % regs_per_chunk else x[r]
            run = part[SUBLANES - 1:SUBLANES, :]
            parts.append(part)
        return jnp.concatenate(parts, axis=0)
    return x.reshape(rows, width)


def _recurrence_fast(qs, kk, g, iv, side, *, states_t, group, chunk, carry_state):
    rows = qs.shape[0]
    n_groups = rows // group
    chunks_per_group = group // chunk
    n_chunks = rows // chunk
    blk_mask = _same_chunk_causal(group, chunk)
    b = _chunk_cumsum(g, chunk)
    side(1)

    iv = iv().astype(_BF16)
    q_dec = (qs * jnp.exp(b)).astype(_BF16)
    k_inv = kk * jnp.exp(-b)
    k_inv_bf = k_inv.astype(_BF16)

    low = None
    for c in range(n_chunks):
        b_last = b[(c + 1) * chunk - 1:(c + 1) * chunk, :]
        low = b_last if low is None else jnp.minimum(low, b_last)

    new_states = [[None] * H_A for _ in range(n_chunks)]
    o_heads = []
    for hd in range(H_A):
        ln = slice(hd * DK, (hd + 1) * DK)
        groups = [slice(gi * group, (gi + 1) * group) for gi in range(n_groups)]
        chunks = [slice(c * chunk, (c + 1) * chunk) for c in range(n_chunks)]
        scores = [jnp.where(blk_mask, _dot(q_dec[rg, ln], k_inv_bf[rg, ln], _NT), 0.0).astype(_BF16)
                  for rg in groups]
        dec_last, grow = [], []
        for c, rw in enumerate(chunks):
            last = (c + 1) * chunk - 1
            dec_last.append(jnp.exp(b[last:last + 1, ln]))
            k_upd = (k_inv[rw, ln] * dec_last[c]).astype(_BF16)
            grow.append(_dot(iv[rw, ln], k_upd, _TN))
        intra = [_dot(sc, iv[rg, ln]) for sc, rg in zip(scores, groups)]
        o_chunks = []
        s_t = states_t[0][hd]
        for c, rw in enumerate(chunks):
            if not carry_state:
                s_t = states_t[c][hd]
            inter = _dot(q_dec[rw, ln], s_t.astype(_BF16), _NT)
            gi, ci = divmod(c, chunks_per_group)
            o_chunks.append(inter + intra[gi][ci * chunk:(ci + 1) * chunk])
            s_t = s_t * dec_last[c] + grow[c]
            new_states[c][hd] = s_t
        o_heads.append(jnp.concatenate(o_chunks, axis=0) if n_chunks > 1 else o_chunks[0])
        side(2 + hd)
    return jnp.concatenate(o_heads, axis=1), (new_states, jnp.min(low))


def _recurrence_exact(qs, kk, g, iv, side, *, work_ref, state_ref, group, chunk):
    rows = qs.shape[0]
    assert chunk % SUBLANES == 0 and group % SUBLANES == 0
    minis_per_chunk = chunk // SUBLANES
    b = _chunk_cumsum(g, SUBLANES)
    for k, a in enumerate((qs, kk, b, iv())):
        work_ref[k, 0:rows, :] = a
    t_idx = lax.broadcasted_iota(jnp.int32, (SUBLANES, DK), 0)

    def mini(m, carry):
        r0 = pl.multiple_of(m * SUBLANES, SUBLANES)
        qs8, kk8, b8, iv8 = (work_ref[k, pl.ds(r0, SUBLANES), :] for k in range(4))
        c = m // minis_per_chunk
        for hd in range(H_A):
            ln = slice(hd * DK, (hd + 1) * DK)
            q_h, k_h, b_h, v_h = qs8[:, ln], kk8[:, ln], b8[:, ln], iv8[:, ln]
            s_t = state_ref[c, hd]
            acc = _dot(q_h * jnp.exp(b_h), s_t, _NT)
            for s in range(SUBLANES):
                dec = jnp.exp(jnp.minimum(b_h - b_h[s:s + 1, :], 0.0))
                term = jnp.where(t_idx >= s, q_h * k_h[s:s + 1, :] * dec, 0.0)
                acc = acc + jnp.sum(term, axis=1, keepdims=True) * v_h[s:s + 1, :]
            work_ref[4, pl.ds(r0, SUBLANES), ln] = acc
            b_end = b_h[SUBLANES - 1:SUBLANES, :]
            k_upd = k_h * jnp.exp(b_end - b_h)
            state_ref[c, hd] = s_t * jnp.exp(b_end) + _dot(v_h, k_upd, _TN)
        return carry

    lax.fori_loop(0, rows // SUBLANES, mini, 0)
    return work_ref[4, 0:rows, :], None


def _layer_block(x, refs, recurrence, *, group, mix_chunk):
    (norm_w_ref, w_in_ref, lb_logits_ref, g_norm_w_ref, ln_v_w_ref, ln_v_b_ref,
     w_s_ref, b_s_ref, w_out_ref, final_w_ref) = refs
    n_groups = x.shape[0] // group

    h = _rms(x, norm_w_ref[...]).astype(_BF16)
    slices = {}

    def proj(k):
        if k not in slices:
            slices[k] = _dot(h, w_in_ref[:, k * D_A:(k + 1) * D_A])
        return slices[k]

    def side(point):
        for k in PROJECTION_PLAN.get(point, ()):
            proj(k)

    logits = lb_logits_ref[...]
    e = jnp.exp(logits - jnp.max(logits, axis=0, keepdims=True))
    lb = e[0:1, :] / jnp.sum(e, axis=0, keepdims=True)

    side(0)
    half_span = 0.5 * (1.0 - lb)
    f = (lb + half_span) + half_span * _half_tanh_half(proj(1))[0]
    qs = _silu(proj(0)) * (DK ** -0.5)
    o, aux = recurrence(qs, 1.0 - f, jnp.log(f), lambda: proj(2), side)

    ga = proj(3)
    y_a_heads = []
    for hd in range(H_A):
        o_h = o[:, hd * DV:(hd + 1) * DV]
        y_a_heads.append(o_h * lax.rsqrt(jnp.mean(o_h * o_h, axis=-1, keepdims=True) + EPS))
    y_a = jnp.concatenate(y_a_heads, axis=1) * g_norm_w_ref[...] * _silu(ga)

    u = _gelu_exact(proj(4))
    v = _gelu_exact(proj(5))
    mu = jnp.mean(v, axis=-1, keepdims=True)
    vc = v - mu
    var = jnp.mean(vc * vc, axis=-1, keepdims=True)
    v_norm = vc * lax.rsqrt(var + EPS) * ln_v_w_ref[...] + ln_v_b_ref[...]
    gb = proj(6)
    v_bf = v_norm.astype(_BF16)
    mix_mask = _same_chunk_causal(group, mix_chunk)
    pos_r = lax.broadcasted_iota(jnp.int32, (group, MLP_CHUNK), 0)
    pos_c = lax.broadcasted_iota(jnp.int32, (group, MLP_CHUNK), 1)
    pos_mask = jnp.bitwise_and(pos_r, mix_chunk - 1) == pos_c
    mix_heads = []
    for hb in range(H_B):
        ln = slice(hb * C_B, (hb + 1) * C_B)
        if group == mix_chunk == MLP_CHUNK:
            w_full = w_s_ref[hb]
        else:
            w_rows = jnp.concatenate([w_s_ref[hb][:mix_chunk]] * (group // mix_chunk), axis=0)
            w_full = _dot(w_rows.astype(_BF16), jnp.where(pos_mask, 1.0, 0.0).astype(_BF16), _NT)
        w = jnp.where(mix_mask, w_full, 0.0).astype(_BF16)
        bias = jnp.sum(jnp.where(pos_mask, b_s_ref[hb:hb + 1, :], 0.0), axis=1, keepdims=True)
        mixed = _dot(w, _lane_cat(v_bf[:, ln], group)) + bias
        mix_heads.append(_lane_split(mixed, n_groups))
    mix = jnp.concatenate(mix_heads, axis=1)
    y_b = u * mix * _silu(gb)

    y = jnp.concatenate([y_a, y_b], axis=1).astype(_BF16)
    part = max(x.shape[0] // OUT_PARTS, group)
    outs = []
    for r0 in range(0, x.shape[0], part):
        out = x[r0:r0 + part] + _dot(y[r0:r0 + part], w_out_ref[...])
        outs.append(_rms(out, final_w_ref[...]))
    return jnp.concatenate(outs, axis=0), v_norm, aux


def _stage_weights_bf16(pairs, stage_ref, sem_ref):
    chunks = [(src, dst, c0) for src, dst in pairs for c0 in range(0, src.shape[1], WEIGHT_STAGE_COLS)]
    slots = stage_ref.shape[0]
    ahead = slots - 1

    def copy(i):
        src, _, c0 = chunks[i]
        return pltpu.make_async_copy(src.at[:, c0:c0 + WEIGHT_STAGE_COLS], stage_ref.at[i % slots], sem_ref.at[i % slots])

    for i in range(min(ahead, len(chunks))):
        copy(i).start()
    for i, (_, dst, c0) in enumerate(chunks):
        if i + ahead < len(chunks):
            copy(i + ahead).start()
        copy(i).wait()
        dst[:, c0:c0 + WEIGHT_STAGE_COLS] = stage_ref[i % slots].astype(_BF16)


def _fused_kernel(xp_ref, xs_hbm, state_in_hbm, norm_w_ref, w_in_hbm, lb_logits_ref, g_norm_w_ref,
                  ln_v_w_ref, ln_v_b_ref, w_s_ref, b_s_ref, w_out_hbm, final_w_ref,
                  yp_ref, state_p_ref, ys_hbm, state_s_hbm, v_hbm,
                  state_t_scr, state_prev_scr, work_scr, w_in_ref, w_out_ref, stage_scr, stage_sem,
                  xs_ref, state_in_ref, ys_ref, state_s_ref, v_ref, io_sem,
                  *, sample_step, last_step, steps_per_stream, n_streams, frames):
    step = pl.program_id(0)
    refs = (norm_w_ref, w_in_ref, lb_logits_ref, g_norm_w_ref, ln_v_w_ref, ln_v_b_ref,
            w_s_ref, b_s_ref, w_out_ref, final_w_ref)
    sample_in = [pltpu.make_async_copy(src, dst, io_sem.at[i])
                 for i, (src, dst) in enumerate([(xs_hbm, xs_ref), (state_in_hbm, state_in_ref)])]
    sample_out = [pltpu.make_async_copy(src, dst, io_sem.at[len(sample_in) + i])
                  for i, (src, dst) in enumerate([(ys_ref, ys_hbm), (state_s_ref, state_s_hbm), (v_ref, v_hbm)])]

    @pl.when(step == 0)
    def _():
        _stage_weights_bf16([(w_in_hbm, w_in_ref), (w_out_hbm, w_out_ref)], stage_scr, stage_sem)
        for copy in sample_in:
            copy.start()

    def too_steep(low):
        return jnp.logical_not(low >= -LOG_DECAY_LIMIT)

    @pl.when(step != sample_step)
    def _prompt():
        j = lax.rem(jnp.where(step > sample_step, step - 1, step), steps_per_stream)

        @pl.when(j == 0)
        def _():
            state_t_scr[...] = jnp.zeros_like(state_t_scr)

        state_prev_scr[...] = state_t_scr[...]
        fast = functools.partial(
            _recurrence_fast, states_t=[[state_t_scr[0, hd] for hd in range(H_A)]],
            group=PROMPT_GROUP, chunk=PROMPT_CHUNK, carry_state=True)
        y, _, (new_states, low) = _layer_block(xp_ref[0], refs, fast, group=PROMPT_GROUP, mix_chunk=MLP_CHUNK)
        yp_ref[0] = y
        for hd in range(H_A):
            state_t_scr[0, hd] = new_states[-1][hd]

        @pl.when(too_steep(low))
        def _():
            state_t_scr[...] = state_prev_scr[...]
            exact = functools.partial(_recurrence_exact, work_ref=work_scr, state_ref=state_t_scr,
                                      group=PROMPT_GROUP, chunk=PROMPT_BLOCK)
            y_exact, _, _ = _layer_block(xp_ref[0], refs, exact, group=PROMPT_GROUP, mix_chunk=MLP_CHUNK)
            yp_ref[0] = y_exact

        @pl.when(j == steps_per_stream - 1)
        def _():
            for hd in range(H_A):
                state_p_ref[0, hd] = state_t_scr[0, hd].T

    @pl.when(step == sample_step)
    def _sample():
        for copy in sample_in:
            copy.wait()
        rows = n_streams * frames
        fast = functools.partial(
            _recurrence_fast,
            states_t=[[state_in_ref[s, hd].T for hd in range(H_A)] for s in range(n_streams)],
            group=rows, chunk=frames, carry_state=False)
        y, v_norm, (new_states, low) = _layer_block(xs_ref[...], refs, fast, group=rows, mix_chunk=frames)
        ys_ref[...] = y
        v_ref[...] = v_norm
        for s in range(n_streams):
            for hd in range(H_A):
                state_s_ref[s, hd] = new_states[s][hd].T

        @pl.when(too_steep(low))
        def _():
            for s in range(n_streams):
                for hd in range(H_A):
                    state_s_ref[s, hd] = state_in_ref[s, hd].T
            exact = functools.partial(_recurrence_exact, work_ref=work_scr, state_ref=state_s_ref,
                                      group=rows, chunk=frames)
            y_exact, v_exact, _ = _layer_block(xs_ref[...], refs, exact, group=rows, mix_chunk=frames)
            ys_ref[...] = y_exact
            v_ref[...] = v_exact
            for s in range(n_streams):
                for hd in range(H_A):
                    state_s_ref[s, hd] = state_s_ref[s, hd].T

        for copy in sample_out:
            copy.start()

    @pl.when(step == last_step)
    def _():
        for copy in sample_out:
            copy.wait()


def _full(shape):
    return pl.BlockSpec(shape, lambda *_: (0,) * len(shape))


def kernel(x_prompt, x_sample, state_hgrn, norm_w, w_in, lb_logits, g_norm_w, ln_v_w, ln_v_b, w_s, b_s, w_out, final_norm_w):
    batch, seq, _ = x_prompt.shape
    dec_batch, dec_seq, _ = x_sample.shape
    assert norm_w.shape[0] == 1, "single-layer trunk"
    assert seq % PROMPT_BLOCK == 0 and PROMPT_BLOCK % PROMPT_GROUP == 0
    assert PROMPT_GROUP % MLP_CHUNK == 0 and MLP_CHUNK % PROMPT_CHUNK == 0
    assert dec_seq <= HGRN_CHUNK and dec_seq <= MLP_CHUNK and dec_seq & (dec_seq - 1) == 0
    assert w_s.shape[2:] == (MLP_CHUNK, MLP_CHUNK)

    rp = PROMPT_BLOCK
    rs = dec_batch * dec_seq
    steps_per_stream = seq // rp
    prompt_steps = batch * steps_per_stream
    row = lambda a: a.reshape(1, -1)

    sample_step = prompt_steps // 2

    def x_map(step):
        blk = jnp.where(step > sample_step, step - 1, step)
        return blk // steps_per_stream, blk % steps_per_stream, 0

    def y_map(step):
        blk = jnp.where(step >= sample_step, step - 1, step)
        return blk // steps_per_stream, blk % steps_per_stream, 0

    def state_map(step):
        return y_map(step)[0], 0, 0, 0

    any_space = pl.BlockSpec(memory_space=pl.ANY)

    y_prompt, state_p, y_sample, state_s, v_sample = pl.pallas_call(
        functools.partial(_fused_kernel, sample_step=sample_step, last_step=prompt_steps,
                          steps_per_stream=steps_per_stream, n_streams=dec_batch, frames=dec_seq),
        grid=(prompt_steps + 1,),
        in_specs=[pl.BlockSpec((1, rp, D_MODEL), x_map), any_space, any_space,
                  _full((1, D_MODEL)), any_space, _full(lb_logits.shape),
                  _full((1, D_A)), _full((1, D_B)), _full((1, D_B)),
                  _full((H_B, MLP_CHUNK, MLP_CHUNK)), _full((H_B, MLP_CHUNK)),
                  any_space, _full((1, D_MODEL))],
        out_specs=[pl.BlockSpec((1, rp, D_MODEL), y_map),
                   pl.BlockSpec((1, H_A, DK, DV), state_map),
                   any_space, any_space, any_space],
        out_shape=[jax.ShapeDtypeStruct((batch, seq, D_MODEL), _F32),
                   jax.ShapeDtypeStruct((batch, H_A, DK, DV), _F32),
                   jax.ShapeDtypeStruct((rs, D_MODEL), _F32),
                   jax.ShapeDtypeStruct((dec_batch, H_A, DK, DV), _F32),
                   jax.ShapeDtypeStruct((rs, D_B), _F32)],
        scratch_shapes=[pltpu.VMEM((1, H_A, DV, DK), _F32),
                        pltpu.VMEM((1, H_A, DV, DK), _F32),
                        pltpu.VMEM((5, max(rp, rs), D_A), _F32),
                        pltpu.VMEM((D_MODEL, IN_WIDTH), _BF16),
                        pltpu.VMEM((D_MODEL, D_MODEL), _BF16),
                        pltpu.VMEM((WEIGHT_STAGE_SLOTS, D_MODEL, WEIGHT_STAGE_COLS), _F32),
                        pltpu.SemaphoreType.DMA((WEIGHT_STAGE_SLOTS,)),
                        pltpu.VMEM((rs, D_MODEL), _F32),
                        pltpu.VMEM((dec_batch, H_A, DK, DV), _F32),
                        pltpu.VMEM((rs, D_MODEL), _F32),
                        pltpu.VMEM((dec_batch, H_A, DK, DV), _F32),
                        pltpu.VMEM((rs, D_B), _F32),
                        pltpu.SemaphoreType.DMA((5,))],
        compiler_params=pltpu.CompilerParams(
            dimension_semantics=("arbitrary",), vmem_limit_bytes=VMEM_LIMIT_BYTES),
        name="layer_step",
    )(x_prompt, x_sample.reshape(rs, D_MODEL), state_hgrn[0],
      row(norm_w[0]), w_in.reshape(D_MODEL, IN_WIDTH), lb_logits, row(g_norm_w[0]), row(ln_v_w[0]), row(ln_v_b[0]),
      w_s[0], b_s[0], w_out.reshape(D_MODEL, D_MODEL), row(final_norm_w))

    return (y_prompt,
            y_sample.reshape(dec_batch, dec_seq, D_MODEL),
            state_p[None],
            state_s[None],
            v_sample.reshape(1, dec_batch, dec_seq, D_B))
```

```python
import functools
import math

import jax
import jax.numpy as jnp
from jax import lax
from jax.experimental import pallas as pl
from jax.experimental.pallas import tpu as pltpu

D_MODEL = 1024
D_A = 512
D_B = 512
H_A = 4
DK = 128
DV = 128
H_B = 4
C_B = 128
MLP_CHUNK = 128
HGRN_CHUNK = 64
IN_WIDTH = 4 * D_A + 3 * D_B
EPS = 1e-6

PROMPT_BLOCK = 512
BLOCKS_PER_STEP = 2
PROMPT_GROUP = 128
PROMPT_CHUNK = 128
SUBLANES = 8
LOG_DECAY_LIMIT = 78.0
VMEM_LIMIT_BYTES = 60 * 1024 * 1024
PROJECTION_PLAN = {0: (1, 0), 1: (2, 4), 3: (5,), 5: (3, 6)}

WEIGHT_STAGE_COLS = 512
WEIGHT_STAGE_SLOTS = 3
OUT_PARTS = 4

_NT = (((1,), (1,)), ((), ()))
_TN = (((0,), (0,)), ((), ()))
_F32 = jnp.float32
_BF16 = jnp.bfloat16


def _dot(a, b, dims=None):
    if dims is None:
        return jnp.dot(a, b, preferred_element_type=_F32)
    return lax.dot_general(a, b, dims, preferred_element_type=_F32)


def _rms(x, w):
    return x * lax.rsqrt(jnp.mean(x * x, axis=-1, keepdims=True) + EPS) * w


def _half_tanh_half(x):
    h = 0.5 * x
    return jnp.tanh(h), h


def _silu(x):
    t, h = _half_tanh_half(x)
    return h * t + h


def _gelu_exact(x):
    return 0.5 * x * (1.0 + lax.erf(x * math.sqrt(0.5)))


def _same_chunk_causal(rows, chunk):
    shift = chunk.bit_length() - 1
    assert chunk == 1 << shift
    t = lax.broadcasted_iota(jnp.int32, (rows, rows), 0)
    s = lax.broadcasted_iota(jnp.int32, (rows, rows), 1)
    same = lax.shift_right_logical(t, shift) == lax.shift_right_logical(s, shift)
    return jnp.logical_and(same, s <= t)


def _lane_cat(a, group):
    n = a.shape[0] // group
    if n == 1:
        return a
    return jnp.concatenate([a[gi * group:(gi + 1) * group] for gi in range(n)], axis=1)


def _lane_split(a, n):
    if n == 1:
        return a
    w = a.shape[1] // n
    return jnp.concatenate([a[:, gi * w:(gi + 1) * w] for gi in range(n)], axis=0)


def _chunk_cumsum(g, chunk):
    rows, width = g.shape
    n_regs = rows // SUBLANES
    x = g.reshape(n_regs, SUBLANES, width)
    sub = lax.broadcasted_iota(jnp.int32, x.shape, 1)
    shift = 1
    while shift < SUBLANES:
        x = x + jnp.where(sub >= shift, pltpu.roll(x, shift, 1), 0.0)
        shift *= 2
    regs_per_chunk = chunk // SUBLANES
    if regs_per_chunk > 1:
        parts, run = [], None
        for r in range(n_regs):
            part = x[r] + run if r % regs_per_chunk else x[r]
            run = part[SUBLANES - 1:SUBLANES, :]
            parts.append(part)
        return jnp.concatenate(parts, axis=0)
    return x.reshape(rows, width)


def _recurrence_fast(qs, kk, g, iv, side, *, states_t, group, chunk, carry_state):
    rows = qs.shape[0]
    n_groups = rows // group
    chunks_per_group = group // chunk
    n_chunks = rows // chunk
    blk_mask = _same_chunk_causal(group, chunk)
    b = _chunk_cumsum(g, chunk)
    side(1)

    iv = iv().astype(_BF16)
    q_dec = (qs * jnp.exp(b)).astype(_BF16)
    k_inv = kk * jnp.exp(-b)
    k_inv_bf = k_inv.astype(_BF16)

    low = None
    for c in range(n_chunks):
        b_last = b[(c + 1) * chunk - 1:(c + 1) * chunk, :]
        low = b_last if low is None else jnp.minimum(low, b_last)

    new_states = [[None] * H_A for _ in range(n_chunks)]
    o_heads = []
    for hd in range(H_A):
        ln = slice(hd * DK, (hd + 1) * DK)
        groups = [slice(gi * group, (gi + 1) * group) for gi in range(n_groups)]
        chunks = [slice(c * chunk, (c + 1) * chunk) for c in range(n_chunks)]
        scores = [jnp.where(blk_mask, _dot(q_dec[rg, ln], k_inv_bf[rg, ln], _NT), 0.0).astype(_BF16)
                  for rg in groups]
        dec_last, grow = [], []
        for c, rw in enumerate(chunks):
            last = (c + 1) * chunk - 1
            dec_last.append(jnp.exp(b[last:last + 1, ln]))
            k_upd = (k_inv[rw, ln] * dec_last[c]).astype(_BF16)
            grow.append(_dot(iv[rw, ln], k_upd, _TN))
        intra = [_dot(sc, iv[rg, ln]) for sc, rg in zip(scores, groups)]
        o_chunks = []
        s_t = states_t[0][hd]
        for c, rw in enumerate(chunks):
            if not carry_state:
                s_t = states_t[c][hd]
            inter = _dot(q_dec[rw, ln], s_t.astype(_BF16), _NT)
            gi, ci = divmod(c, chunks_per_group)
            o_chunks.append(inter + intra[gi][ci * chunk:(ci + 1) * chunk])
            s_t = s_t * dec_last[c] + grow[c]
            new_states[c][hd] = s_t
        o_heads.append(jnp.concatenate(o_chunks, axis=0) if n_chunks > 1 else o_chunks[0])
        side(2 + hd)
    return jnp.concatenate(o_heads, axis=1), (new_states, jnp.min(low))


def _recurrence_exact(qs, kk, g, iv, side, *, work_ref, state_ref, group, chunk):
    rows = qs.shape[0]
    assert chunk % SUBLANES == 0 and group % SUBLANES == 0
    minis_per_chunk = chunk // SUBLANES
    b = _chunk_cumsum(g, SUBLANES)
    half = work_ref.shape[1] // 2
    assert rows <= half and work_ref.shape[0] * 2 >= 5

    def staged(k, r0, n):
        return work_ref.at[k // 2, pl.ds((k % 2) * half + r0, n), :]

    for k, a in enumerate((qs, kk, b, iv())):
        staged(k, 0, rows)[...] = a
    t_idx = lax.broadcasted_iota(jnp.int32, (SUBLANES, DK), 0)

    def mini(m, carry):
        r0 = pl.multiple_of(m * SUBLANES, SUBLANES)
        qs8, kk8, b8, iv8 = (staged(k, r0, SUBLANES)[...] for k in range(4))
        c = m // minis_per_chunk
        for hd in range(H_A):
            ln = slice(hd * DK, (hd + 1) * DK)
            q_h, k_h, b_h, v_h = qs8[:, ln], kk8[:, ln], b8[:, ln], iv8[:, ln]
            s_t = state_ref[c, hd]
            acc = _dot(q_h * jnp.exp(b_h), s_t, _NT)
            for s in range(SUBLANES):
                dec = jnp.exp(jnp.minimum(b_h - b_h[s:s + 1, :], 0.0))
                term = jnp.where(t_idx >= s, q_h * k_h[s:s + 1, :] * dec, 0.0)
                acc = acc + jnp.sum(term, axis=1, keepdims=True) * v_h[s:s + 1, :]
            staged(4, r0, SUBLANES)[:, ln] = acc
            b_end = b_h[SUBLANES - 1:SUBLANES, :]
            k_upd = k_h * jnp.exp(b_end - b_h)
            state_ref[c, hd] = s_t * jnp.exp(b_end) + _dot(v_h, k_upd, _TN)
        return carry

    lax.fori_loop(0, rows // SUBLANES, mini, 0)
    return staged(4, 0, rows)[...], None


def _layer_block(x, refs, recurrence, *, group, mix_chunk):
    (norm_w_ref, w_in_ref, lb_logits_ref, g_norm_w_ref, ln_v_w_ref, ln_v_b_ref,
     w_s_ref, b_s_ref, w_out_ref, final_w_ref) = refs
    n_groups = x.shape[0] // group

    h = _rms(x, norm_w_ref[...]).astype(_BF16)
    slices = {}

    def proj(k):
        if k not in slices:
            slices[k] = _dot(h, w_in_ref[:, k * D_A:(k + 1) * D_A])
        return slices[k]

    def side(point):
        for k in PROJECTION_PLAN.get(point, ()):
            proj(k)

    logits = lb_logits_ref[...]
    e = jnp.exp(logits - jnp.max(logits, axis=0, keepdims=True))
    lb = e[0:1, :] / jnp.sum(e, axis=0, keepdims=True)

    side(0)
    half_span = 0.5 * (1.0 - lb)
    f = (lb + half_span) + half_span * _half_tanh_half(proj(1))[0]
    qs = _silu(proj(0)) * (DK ** -0.5)
    o, aux = recurrence(qs, 1.0 - f, jnp.log(f), lambda: proj(2), side)

    ga = proj(3)
    y_a_heads = []
    for hd in range(H_A):
        o_h = o[:, hd * DV:(hd + 1) * DV]
        y_a_heads.append(o_h * lax.rsqrt(jnp.mean(o_h * o_h, axis=-1, keepdims=True) + EPS))
    y_a = jnp.concatenate(y_a_heads, axis=1) * g_norm_w_ref[...] * _silu(ga)

    u = _gelu_exact(proj(4))
    v = _gelu_exact(proj(5))
    mu = jnp.mean(v, axis=-1, keepdims=True)
    vc = v - mu
    var = jnp.mean(vc * vc, axis=-1, keepdims=True)
    v_norm = vc * lax.rsqrt(var + EPS) * ln_v_w_ref[...] + ln_v_b_ref[...]
    gb = proj(6)
    v_bf = v_norm.astype(_BF16)
    mix_mask = _same_chunk_causal(group, mix_chunk)
    pos_r = lax.broadcasted_iota(jnp.int32, (group, MLP_CHUNK), 0)
    pos_c = lax.broadcasted_iota(jnp.int32, (group, MLP_CHUNK), 1)
    pos_mask = jnp.bitwise_and(pos_r, mix_chunk - 1) == pos_c
    mix_heads = []
    for hb in range(H_B):
        ln = slice(hb * C_B, (hb + 1) * C_B)
        if group == mix_chunk == MLP_CHUNK:
            w_full = w_s_ref[hb]
        else:
            w_rows = jnp.concatenate([w_s_ref[hb][:mix_chunk]] * (group // mix_chunk), axis=0)
            w_full = _dot(w_rows.astype(_BF16), jnp.where(pos_mask, 1.0, 0.0).astype(_BF16), _NT)
        w = jnp.where(mix_mask, w_full, 0.0).astype(_BF16)
        bias = jnp.sum(jnp.where(pos_mask, b_s_ref[hb:hb + 1, :], 0.0), axis=1, keepdims=True)
        mixed = _dot(w, _lane_cat(v_bf[:, ln], group)) + bias
        mix_heads.append(_lane_split(mixed, n_groups))
    mix = jnp.concatenate(mix_heads, axis=1)
    y_b = u * mix * _silu(gb)

    y = jnp.concatenate([y_a, y_b], axis=1).astype(_BF16)
    part = max(x.shape[0] // OUT_PARTS, group)
    outs = []
    for r0 in range(0, x.shape[0], part):
        out = x[r0:r0 + part] + _dot(y[r0:r0 + part], w_out_ref[...])
        outs.append(_rms(out, final_w_ref[...]))
    return jnp.concatenate(outs, axis=0), v_norm, aux


def _stage_weights_bf16(pairs, stage_ref, sem_ref):
    chunks = [(src, dst, c0) for src, dst in pairs for c0 in range(0, src.shape[1], WEIGHT_STAGE_COLS)]
    slots = stage_ref.shape[0]
    ahead = slots - 1

    def copy(i):
        src, _, c0 = chunks[i]
        return pltpu.make_async_copy(src.at[:, c0:c0 + WEIGHT_STAGE_COLS], stage_ref.at[i % slots], sem_ref.at[i % slots])

    for i in range(min(ahead, len(chunks))):
        copy(i).start()
    for i, (_, dst, c0) in enumerate(chunks):
        if i + ahead < len(chunks):
            copy(i + ahead).start()
        copy(i).wait()
        dst[:, c0:c0 + WEIGHT_STAGE_COLS] = stage_ref[i % slots].astype(_BF16)


def _fused_kernel(xp_ref, xs_hbm, state_in_hbm, norm_w_ref, w_in_hbm, lb_logits_ref, g_norm_w_ref,
                  ln_v_w_ref, ln_v_b_ref, w_s_ref, b_s_ref, w_out_hbm, final_w_ref,
                  yp_ref, state_p_ref, ys_hbm, state_s_hbm, v_hbm,
                  state_t_scr, state_prev_scr, w_in_ref, w_out_ref, stage_scr, stage_sem,
                  xs_ref, state_in_ref, ys_ref, state_s_ref, v_ref, io_sem,
                  *, sample_step, last_step, steps_per_stream, n_streams, frames):
    step = pl.program_id(0)
    refs = (norm_w_ref, w_in_ref, lb_logits_ref, g_norm_w_ref, ln_v_w_ref, ln_v_b_ref,
            w_s_ref, b_s_ref, w_out_ref, final_w_ref)
    sample_in = [pltpu.make_async_copy(src, dst, io_sem.at[i])
                 for i, (src, dst) in enumerate([(xs_hbm, xs_ref), (state_in_hbm, state_in_ref)])]
    sample_out = [pltpu.make_async_copy(src, dst, io_sem.at[len(sample_in) + i])
                  for i, (src, dst) in enumerate([(ys_ref, ys_hbm), (state_s_ref, state_s_hbm), (v_ref, v_hbm)])]

    @pl.when(step == 0)
    def _():
        _stage_weights_bf16([(w_in_hbm, w_in_ref), (w_out_hbm, w_out_ref)], stage_scr, stage_sem)
        for copy in sample_in:
            copy.start()

    def too_steep(low):
        return jnp.logical_not(low >= -LOG_DECAY_LIMIT)

    @pl.when(step != sample_step)
    def _prompt():
        j = lax.rem(jnp.where(step > sample_step, step - 1, step), steps_per_stream)

        @pl.when(j == 0)
        def _():
            state_t_scr[...] = jnp.zeros_like(state_t_scr)

        state_prev_scr[...] = state_t_scr[...]
        states = [state_t_scr[0, hd] for hd in range(H_A)]
        low = None
        for blk in range(BLOCKS_PER_STEP):
            rows = slice(blk * PROMPT_BLOCK, (blk + 1) * PROMPT_BLOCK)
            fast = functools.partial(_recurrence_fast, states_t=[states], group=PROMPT_GROUP,
                                     chunk=PROMPT_CHUNK, carry_state=True)
            y, _, (new_states, low_blk) = _layer_block(xp_ref[0, rows, :], refs, fast,
                                                       group=PROMPT_GROUP, mix_chunk=MLP_CHUNK)
            yp_ref[0, rows, :] = y
            states = new_states[-1]
            low = low_blk if low is None else jnp.minimum(low, low_blk)
        for hd in range(H_A):
            state_t_scr[0, hd] = states[hd]

        @pl.when(too_steep(low))
        def _():
            state_t_scr[...] = state_prev_scr[...]
            exact = functools.partial(_recurrence_exact, work_ref=stage_scr, state_ref=state_t_scr,
                                      group=PROMPT_GROUP, chunk=PROMPT_BLOCK)
            for blk in range(BLOCKS_PER_STEP):
                rows = slice(blk * PROMPT_BLOCK, (blk + 1) * PROMPT_BLOCK)
                y_exact, _, _ = _layer_block(xp_ref[0, rows, :], refs, exact,
                                             group=PROMPT_GROUP, mix_chunk=MLP_CHUNK)
                yp_ref[0, rows, :] = y_exact

        @pl.when(j == steps_per_stream - 1)
        def _():
            for hd in range(H_A):
                state_p_ref[0, hd] = state_t_scr[0, hd].T

    @pl.when(step == sample_step)
    def _sample():
        for copy in sample_in:
            copy.wait()
        rows = n_streams * frames
        fast = functools.partial(
            _recurrence_fast,
            states_t=[[state_in_ref[s, hd].T for hd in range(H_A)] for s in range(n_streams)],
            group=rows, chunk=frames, carry_state=False)
        y, v_norm, (new_states, low) = _layer_block(xs_ref[...], refs, fast, group=rows, mix_chunk=frames)
        ys_ref[...] = y
        v_ref[...] = v_norm
        for s in range(n_streams):
            for hd in range(H_A):
                state_s_ref[s, hd] = new_states[s][hd].T

        @pl.when(too_steep(low))
        def _():
            for s in range(n_streams):
                for hd in range(H_A):
                    state_s_ref[s, hd] = state_in_ref[s, hd].T
            exact = functools.partial(_recurrence_exact, work_ref=stage_scr, state_ref=state_s_ref,
                                      group=rows, chunk=frames)
            y_exact, v_exact, _ = _layer_block(xs_ref[...], refs, exact, group=rows, mix_chunk=frames)
            ys_ref[...] = y_exact
            v_ref[...] = v_exact
            for s in range(n_streams):
                for hd in range(H_A):
                    state_s_ref[s, hd] = state_s_ref[s, hd].T

        for copy in sample_out:
            copy.start()

    @pl.when(step == last_step)
    def _():
        for copy in sample_out:
            copy.wait()


def _full(shape):
    return pl.BlockSpec(shape, lambda *_: (0,) * len(shape))


def kernel(x_prompt, x_sample, state_hgrn, norm_w, w_in, lb_logits, g_norm_w, ln_v_w, ln_v_b, w_s, b_s, w_out, final_norm_w):
    batch, seq, _ = x_prompt.shape
    dec_batch, dec_seq, _ = x_sample.shape
    assert norm_w.shape[0] == 1, "single-layer trunk"
    assert seq % (PROMPT_BLOCK * BLOCKS_PER_STEP) == 0 and PROMPT_BLOCK % PROMPT_GROUP == 0
    assert PROMPT_GROUP % MLP_CHUNK == 0 and MLP_CHUNK % PROMPT_CHUNK == 0
    assert dec_seq <= HGRN_CHUNK and dec_seq <= MLP_CHUNK and dec_seq & (dec_seq - 1) == 0
    assert w_s.shape[2:] == (MLP_CHUNK, MLP_CHUNK)

    rp = PROMPT_BLOCK * BLOCKS_PER_STEP
    rs = dec_batch * dec_seq
    steps_per_stream = seq // rp
    prompt_steps = batch * steps_per_stream
    row = lambda a: a.reshape(1, -1)

    sample_step = prompt_steps // 2

    def x_map(step):
        blk = jnp.where(step > sample_step, step - 1, step)
        return blk // steps_per_stream, blk % steps_per_stream, 0

    def y_map(step):
        blk = jnp.where(step >= sample_step, step - 1, step)
        return blk // steps_per_stream, blk % steps_per_stream, 0

    def state_map(step):
        return y_map(step)[0], 0, 0, 0

    any_space = pl.BlockSpec(memory_space=pl.ANY)

    y_prompt, state_p, y_sample, state_s, v_sample = pl.pallas_call(
        functools.partial(_fused_kernel, sample_step=sample_step, last_step=prompt_steps,
                          steps_per_stream=steps_per_stream, n_streams=dec_batch, frames=dec_seq),
        grid=(prompt_steps + 1,),
        in_specs=[pl.BlockSpec((1, rp, D_MODEL), x_map), any_space, any_space,
                  _full((1, D_MODEL)), any_space, _full(lb_logits.shape),
                  _full((1, D_A)), _full((1, D_B)), _full((1, D_B)),
                  _full((H_B, MLP_CHUNK, MLP_CHUNK)), _full((H_B, MLP_CHUNK)),
                  any_space, _full((1, D_MODEL))],
        out_specs=[pl.BlockSpec((1, rp, D_MODEL), y_map),
                   pl.BlockSpec((1, H_A, DK, DV), state_map),
                   any_space, any_space, any_space],
        out_shape=[jax.ShapeDtypeStruct((batch, seq, D_MODEL), _F32),
                   jax.ShapeDtypeStruct((batch, H_A, DK, DV), _F32),
                   jax.ShapeDtypeStruct((rs, D_MODEL), _F32),
                   jax.ShapeDtypeStruct((dec_batch, H_A, DK, DV), _F32),
                   jax.ShapeDtypeStruct((rs, D_B), _F32)],
        scratch_shapes=[pltpu.VMEM((1, H_A, DV, DK), _F32),
                        pltpu.VMEM((1, H_A, DV, DK), _F32),
                        pltpu.VMEM((D_MODEL, IN_WIDTH), _BF16),
                        pltpu.VMEM((D_MODEL, D_MODEL), _BF16),
                        pltpu.VMEM((WEIGHT_STAGE_SLOTS, D_MODEL, WEIGHT_STAGE_COLS), _F32),
                        pltpu.SemaphoreType.DMA((WEIGHT_STAGE_SLOTS,)),
                        pltpu.VMEM((rs, D_MODEL), _F32),
                        pltpu.VMEM((dec_batch, H_A, DK, DV), _F32),
                        pltpu.VMEM((rs, D_MODEL), _F32),
                        pltpu.VMEM((dec_batch, H_A, DK, DV), _F32),
                        pltpu.VMEM((rs, D_B), _F32),
                        pltpu.SemaphoreType.DMA((5,))],
        compiler_params=pltpu.CompilerParams(
            dimension_semantics=("arbitrary",), vmem_limit_bytes=VMEM_LIMIT_BYTES),
        name="layer_step",
    )(x_prompt, x_sample.reshape(rs, D_MODEL), state_hgrn[0],
      row(norm_w[0]), w_in.reshape(D_MODEL, IN_WIDTH), lb_logits, row(g_norm_w[0]), row(ln_v_w[0]), row(ln_v_b[0]),
      w_s[0], b_s[0], w_out.reshape(D_MODEL, D_MODEL), row(final_norm_w))

    return (y_prompt,
            y_sample.reshape(dec_batch, dec_seq, D_MODEL),
            state_p[None],
            state_s[None],
            v_sample.reshape(1, dec_batch, dec_seq, D_B))
```

```python
import functools
import math

import jax
import jax.numpy as jnp
from jax import lax
from jax.experimental import pallas as pl
from jax.experimental.pallas import tpu as pltpu

D_MODEL = 1024
D_A = 512
D_B = 512
H_A = 4
DK = 128
DV = 128
H_B = 4
C_B = 128
MLP_CHUNK = 128
HGRN_CHUNK = 64
IN_WIDTH = 4 * D_A + 3 * D_B
EPS = 1e-6

PROMPT_BLOCK = 512
BLOCKS_PER_STEP = 2
PROMPT_GROUP = 128
PROMPT_CHUNK = 128
SUBLANES = 8
LOG_DECAY_LIMIT = 78.0
VMEM_LIMIT_BYTES = 60 * 1024 * 1024
PROJECTION_PLAN = {0: (1, 0), 1: (2, 4), 3: (5,), 5: (3, 6)}

WEIGHT_STAGE_COLS = 256
WEIGHT_STAGE_SLABS = 3
OUT_PARTS = 4

_NT = (((1,), (1,)), ((), ()))
_TN = (((0,), (0,)), ((), ()))
_F32 = jnp.float32
_BF16 = jnp.bfloat16


def _dot(a, b, dims=None):
    if dims is None:
        return jnp.dot(a, b, preferred_element_type=_F32)
    return lax.dot_general(a, b, dims, preferred_element_type=_F32)


def _rms(x, w):
    return x * lax.rsqrt(jnp.mean(x * x, axis=-1, keepdims=True) + EPS) * w


def _half_tanh_half(x):
    h = 0.5 * x
    return jnp.tanh(h), h


def _silu(x):
    t, h = _half_tanh_half(x)
    return h * t + h


def _gelu_exact(x):
    return 0.5 * x * (1.0 + lax.erf(x * math.sqrt(0.5)))


def _same_chunk_causal(rows, chunk):
    shift = chunk.bit_length() - 1
    assert chunk == 1 << shift
    t = lax.broadcasted_iota(jnp.int32, (rows, rows), 0)
    s = lax.broadcasted_iota(jnp.int32, (rows, rows), 1)
    same = lax.shift_right_logical(t, shift) == lax.shift_right_logical(s, shift)
    return jnp.logical_and(same, s <= t)


def _lane_cat(a, group):
    n = a.shape[0] // group
    if n == 1:
        return a
    return jnp.concatenate([a[gi * group:(gi + 1) * group] for gi in range(n)], axis=1)


def _lane_split(a, n):
    if n == 1:
        return a
    w = a.shape[1] // n
    return jnp.concatenate([a[:, gi * w:(gi + 1) * w] for gi in range(n)], axis=0)


def _chunk_cumsum(g, chunk):
    rows, width = g.shape
    n_regs = rows // SUBLANES
    x = g.reshape(n_regs, SUBLANES, width)
    sub = lax.broadcasted_iota(jnp.int32, x.shape, 1)
    shift = 1
    while shift < SUBLANES:
        x = x + jnp.where(sub >= shift, pltpu.roll(x, shift, 1), 0.0)
        shift *= 2
    regs_per_chunk = chunk // SUBLANES
    if regs_per_chunk > 1:
        parts, run = [], None
        for r in range(n_regs):
            part = x[r] + run if r % regs_per_chunk else x[r]
            run = part[SUBLANES - 1:SUBLANES, :]
            parts.append(part)
        return jnp.concatenate(parts, axis=0)
    return x.reshape(rows, width)


def _recurrence_fast(qs, kk, g, iv, side, *, states_t, group, chunk, carry_state):
    rows = qs.shape[0]
    n_groups = rows // group
    chunks_per_group = group // chunk
    n_chunks = rows // chunk
    blk_mask = _same_chunk_causal(group, chunk)
    b = _chunk_cumsum(g, chunk)
    side(1)

    iv = iv().astype(_BF16)
    q_dec = (qs * jnp.exp(b)).astype(_BF16)
    k_inv = kk * jnp.exp(-b)
    k_inv_bf = k_inv.astype(_BF16)

    low = None
    for c in range(n_chunks):
        b_last = b[(c + 1) * chunk - 1:(c + 1) * chunk, :]
        low = b_last if low is None else jnp.minimum(low, b_last)

    new_states = [[None] * H_A for _ in range(n_chunks)]
    o_heads = []
    for hd in range(H_A):
        ln = slice(hd * DK, (hd + 1) * DK)
        groups = [slice(gi * group, (gi + 1) * group) for gi in range(n_groups)]
        chunks = [slice(c * chunk, (c + 1) * chunk) for c in range(n_chunks)]
        scores = [jnp.where(blk_mask, _dot(q_dec[rg, ln], k_inv_bf[rg, ln], _NT), 0.0).astype(_BF16)
                  for rg in groups]
        dec_last, grow = [], []
        for c, rw in enumerate(chunks):
            last = (c + 1) * chunk - 1
            dec_last.append(jnp.exp(b[last:last + 1, ln]))
            k_upd = (k_inv[rw, ln] * dec_last[c]).astype(_BF16)
            grow.append(_dot(iv[rw, ln], k_upd, _TN))
        intra = [_dot(sc, iv[rg, ln]) for sc, rg in zip(scores, groups)]
        o_chunks = []
        s_t = states_t[0][hd]
        for c, rw in enumerate(chunks):
            if not carry_state:
                s_t = states_t[c][hd]
            inter = _dot(q_dec[rw, ln], s_t.astype(_BF16), _NT)
            gi, ci = divmod(c, chunks_per_group)
            o_chunks.append(inter + intra[gi][ci * chunk:(ci + 1) * chunk])
            s_t = s_t * dec_last[c] + grow[c]
            new_states[c][hd] = s_t
        o_heads.append(jnp.concatenate(o_chunks, axis=0) if n_chunks > 1 else o_chunks[0])
        side(2 + hd)
    return jnp.concatenate(o_heads, axis=1), (new_states, jnp.min(low))


def _recurrence_exact(qs, kk, g, iv, side, *, work_ref, state_ref, group, chunk):
    rows = qs.shape[0]
    assert chunk % SUBLANES == 0 and group % SUBLANES == 0
    minis_per_chunk = chunk // SUBLANES
    b = _chunk_cumsum(g, SUBLANES)
    half = work_ref.shape[1] // 2
    assert rows <= half and work_ref.shape[0] * 2 >= 5

    def staged(k, r0, n):
        return work_ref.at[k // 2, pl.ds((k % 2) * half + r0, n), :]

    for k, a in enumerate((qs, kk, b, iv())):
        staged(k, 0, rows)[...] = a
    t_idx = lax.broadcasted_iota(jnp.int32, (SUBLANES, DK), 0)

    def mini(m, carry):
        r0 = pl.multiple_of(m * SUBLANES, SUBLANES)
        qs8, kk8, b8, iv8 = (staged(k, r0, SUBLANES)[...] for k in range(4))
        c = m // minis_per_chunk
        for hd in range(H_A):
            ln = slice(hd * DK, (hd + 1) * DK)
            q_h, k_h, b_h, v_h = qs8[:, ln], kk8[:, ln], b8[:, ln], iv8[:, ln]
            s_t = state_ref[c, hd]
            acc = _dot(q_h * jnp.exp(b_h), s_t, _NT)
            for s in range(SUBLANES):
                dec = jnp.exp(jnp.minimum(b_h - b_h[s:s + 1, :], 0.0))
                term = jnp.where(t_idx >= s, q_h * k_h[s:s + 1, :] * dec, 0.0)
                acc = acc + jnp.sum(term, axis=1, keepdims=True) * v_h[s:s + 1, :]
            staged(4, r0, SUBLANES)[:, ln] = acc
            b_end = b_h[SUBLANES - 1:SUBLANES, :]
            k_upd = k_h * jnp.exp(b_end - b_h)
            state_ref[c, hd] = s_t * jnp.exp(b_end) + _dot(v_h, k_upd, _TN)
        return carry

    lax.fori_loop(0, rows // SUBLANES, mini, 0)
    return staged(4, 0, rows)[...], None


def _layer_block(x, refs, recurrence, *, group, mix_chunk):
    (norm_w_ref, w_in_ref, lb_logits_ref, g_norm_w_ref, ln_v_w_ref, ln_v_b_ref,
     w_s_ref, b_s_ref, w_out_ref, final_w_ref) = refs
    n_groups = x.shape[0] // group

    h = _rms(x, norm_w_ref[...]).astype(_BF16)
    slices = {}

    def proj(k):
        if k not in slices:
            slices[k] = _dot(h, w_in_ref[:, k * D_A:(k + 1) * D_A])
        return slices[k]

    def side(point):
        for k in PROJECTION_PLAN.get(point, ()):
            proj(k)

    logits = lb_logits_ref[...]
    e = jnp.exp(logits - jnp.max(logits, axis=0, keepdims=True))
    lb = e[0:1, :] / jnp.sum(e, axis=0, keepdims=True)

    side(0)
    half_span = 0.5 * (1.0 - lb)
    f = (lb + half_span) + half_span * _half_tanh_half(proj(1))[0]
    qs = _silu(proj(0)) * (DK ** -0.5)
    o, aux = recurrence(qs, 1.0 - f, jnp.log(f), lambda: proj(2), side)

    ga = proj(3)
    y_a_heads = []
    for hd in range(H_A):
        o_h = o[:, hd * DV:(hd + 1) * DV]
        y_a_heads.append(o_h * lax.rsqrt(jnp.mean(o_h * o_h, axis=-1, keepdims=True) + EPS))
    y_a = jnp.concatenate(y_a_heads, axis=1) * g_norm_w_ref[...] * _silu(ga)

    u = _gelu_exact(proj(4))
    v = _gelu_exact(proj(5))
    mu = jnp.mean(v, axis=-1, keepdims=True)
    vc = v - mu
    var = jnp.mean(vc * vc, axis=-1, keepdims=True)
    v_norm = vc * lax.rsqrt(var + EPS) * ln_v_w_ref[...] + ln_v_b_ref[...]
    gb = proj(6)
    v_bf = v_norm.astype(_BF16)
    mix_mask = _same_chunk_causal(group, mix_chunk)
    pos_r = lax.broadcasted_iota(jnp.int32, (group, MLP_CHUNK), 0)
    pos_c = lax.broadcasted_iota(jnp.int32, (group, MLP_CHUNK), 1)
    pos_mask = jnp.bitwise_and(pos_r, mix_chunk - 1) == pos_c
    mix_heads = []
    for hb in range(H_B):
        ln = slice(hb * C_B, (hb + 1) * C_B)
        if group == mix_chunk == MLP_CHUNK:
            w_full = w_s_ref[hb]
        else:
            w_rows = jnp.concatenate([w_s_ref[hb][:mix_chunk]] * (group // mix_chunk), axis=0)
            w_full = _dot(w_rows.astype(_BF16), jnp.where(pos_mask, 1.0, 0.0).astype(_BF16), _NT)
        w = jnp.where(mix_mask, w_full, 0.0).astype(_BF16)
        bias = jnp.sum(jnp.where(pos_mask, b_s_ref[hb:hb + 1, :], 0.0), axis=1, keepdims=True)
        mixed = _dot(w, _lane_cat(v_bf[:, ln], group)) + bias
        mix_heads.append(_lane_split(mixed, n_groups))
    mix = jnp.concatenate(mix_heads, axis=1)
    y_b = u * mix * _silu(gb)

    y = jnp.concatenate([y_a, y_b], axis=1).astype(_BF16)
    part = max(x.shape[0] // OUT_PARTS, group)
    outs = []
    for r0 in range(0, x.shape[0], part):
        out = x[r0:r0 + part] + _dot(y[r0:r0 + part], w_out_ref[...])
        outs.append(_rms(out, final_w_ref[...]))
    return jnp.concatenate(outs, axis=0), v_norm, aux


def _stage_weights_bf16(pairs, stage_ref, sem_ref):
    chunks = [(src, dst, c0) for src, dst in pairs for c0 in range(0, src.shape[1], WEIGHT_STAGE_COLS)]
    per_slab = stage_ref.shape[2] // WEIGHT_STAGE_COLS
    slots = stage_ref.shape[0] * per_slab
    ahead = slots - 1

    def slot(i):
        s = i % slots
        return stage_ref.at[s // per_slab, :, pl.ds((s % per_slab) * WEIGHT_STAGE_COLS, WEIGHT_STAGE_COLS)]

    def copy(i):
        src, _, c0 = chunks[i]
        return pltpu.make_async_copy(src.at[:, c0:c0 + WEIGHT_STAGE_COLS], slot(i), sem_ref.at[i % slots])

    for i in range(min(ahead, len(chunks))):
        copy(i).start()
    for i, (_, dst, c0) in enumerate(chunks):
        if i + ahead < len(chunks):
            copy(i + ahead).start()
        copy(i).wait()
        dst[:, c0:c0 + WEIGHT_STAGE_COLS] = slot(i)[...].astype(_BF16)


def _fused_kernel(xp_ref, xs_hbm, state_in_hbm, norm_w_ref, w_in_hbm, lb_logits_ref, g_norm_w_ref,
                  ln_v_w_ref, ln_v_b_ref, w_s_ref, b_s_ref, w_out_hbm, final_w_ref,
                  yp_ref, state_p_ref, ys_hbm, state_s_hbm, v_hbm,
                  state_t_scr, state_prev_scr, w_in_ref, w_out_ref, stage_scr, stage_sem,
                  xs_ref, state_in_ref, ys_ref, state_s_ref, v_ref, io_sem,
                  *, sample_step, last_step, steps_per_stream, n_streams, frames):
    step = pl.program_id(0)
    refs = (norm_w_ref, w_in_ref, lb_logits_ref, g_norm_w_ref, ln_v_w_ref, ln_v_b_ref,
            w_s_ref, b_s_ref, w_out_ref, final_w_ref)
    sample_in = [pltpu.make_async_copy(src, dst, io_sem.at[i])
                 for i, (src, dst) in enumerate([(xs_hbm, xs_ref), (state_in_hbm, state_in_ref)])]
    sample_out = [pltpu.make_async_copy(src, dst, io_sem.at[len(sample_in) + i])
                  for i, (src, dst) in enumerate([(ys_ref, ys_hbm), (state_s_ref, state_s_hbm), (v_ref, v_hbm)])]

    @pl.when(step == 0)
    def _():
        _stage_weights_bf16([(w_in_hbm, w_in_ref), (w_out_hbm, w_out_ref)], stage_scr, stage_sem)
        for copy in sample_in:
            copy.start()

    def too_steep(low):
        return jnp.logical_not(low >= -LOG_DECAY_LIMIT)

    @pl.when(step != sample_step)
    def _prompt():
        j = lax.rem(jnp.where(step > sample_step, step - 1, step), steps_per_stream)

        @pl.when(j == 0)
        def _():
            state_t_scr[...] = jnp.zeros_like(state_t_scr)

        state_prev_scr[...] = state_t_scr[...]
        states = [state_t_scr[0, hd] for hd in range(H_A)]
        low = None
        for blk in range(BLOCKS_PER_STEP):
            rows = slice(blk * PROMPT_BLOCK, (blk + 1) * PROMPT_BLOCK)
            fast = functools.partial(_recurrence_fast, states_t=[states], group=PROMPT_GROUP,
                                     chunk=PROMPT_CHUNK, carry_state=True)
            y, _, (new_states, low_blk) = _layer_block(xp_ref[0, rows, :], refs, fast,
                                                       group=PROMPT_GROUP, mix_chunk=MLP_CHUNK)
            yp_ref[0, rows, :] = y
            states = new_states[-1]
            low = low_blk if low is None else jnp.minimum(low, low_blk)
        for hd in range(H_A):
            state_t_scr[0, hd] = states[hd]

        @pl.when(too_steep(low))
        def _():
            state_t_scr[...] = state_prev_scr[...]
            exact = functools.partial(_recurrence_exact, work_ref=stage_scr, state_ref=state_t_scr,
                                      group=PROMPT_GROUP, chunk=PROMPT_BLOCK)
            for blk in range(BLOCKS_PER_STEP):
                rows = slice(blk * PROMPT_BLOCK, (blk + 1) * PROMPT_BLOCK)
                y_exact, _, _ = _layer_block(xp_ref[0, rows, :], refs, exact,
                                             group=PROMPT_GROUP, mix_chunk=MLP_CHUNK)
                yp_ref[0, rows, :] = y_exact

        @pl.when(j == steps_per_stream - 1)
        def _():
            for hd in range(H_A):
                state_p_ref[0, hd] = state_t_scr[0, hd].T

    @pl.when(step == sample_step)
    def _sample():
        for copy in sample_in:
            copy.wait()
        rows = n_streams * frames
        fast = functools.partial(
            _recurrence_fast,
            states_t=[[state_in_ref[s, hd].T for hd in range(H_A)] for s in range(n_streams)],
            group=rows, chunk=frames, carry_state=False)
        y, v_norm, (new_states, low) = _layer_block(xs_ref[...], refs, fast, group=rows, mix_chunk=frames)
        ys_ref[...] = y
        v_ref[...] = v_norm
        for s in range(n_streams):
            for hd in range(H_A):
                state_s_ref[s, hd] = new_states[s][hd].T

        @pl.when(too_steep(low))
        def _():
            for s in range(n_streams):
                for hd in range(H_A):
                    state_s_ref[s, hd] = state_in_ref[s, hd].T
            exact = functools.partial(_recurrence_exact, work_ref=stage_scr, state_ref=state_s_ref,
                                      group=rows, chunk=frames)
            y_exact, v_exact, _ = _layer_block(xs_ref[...], refs, exact, group=rows, mix_chunk=frames)
            ys_ref[...] = y_exact
            v_ref[...] = v_exact
            for s in range(n_streams):
                for hd in range(H_A):
                    state_s_ref[s, hd] = state_s_ref[s, hd].T

        for copy in sample_out:
            copy.start()

    @pl.when(step == last_step)
    def _():
        for copy in sample_out:
            copy.wait()


def _full(shape):
    return pl.BlockSpec(shape, lambda *_: (0,) * len(shape))


def kernel(x_prompt, x_sample, state_hgrn, norm_w, w_in, lb_logits, g_norm_w, ln_v_w, ln_v_b, w_s, b_s, w_out, final_norm_w):
    batch, seq, _ = x_prompt.shape
    dec_batch, dec_seq, _ = x_sample.shape
    assert norm_w.shape[0] == 1, "single-layer trunk"
    assert seq % (PROMPT_BLOCK * BLOCKS_PER_STEP) == 0 and PROMPT_BLOCK % PROMPT_GROUP == 0
    assert PROMPT_GROUP % MLP_CHUNK == 0 and MLP_CHUNK % PROMPT_CHUNK == 0
    assert dec_seq <= HGRN_CHUNK and dec_seq <= MLP_CHUNK and dec_seq & (dec_seq - 1) == 0
    assert w_s.shape[2:] == (MLP_CHUNK, MLP_CHUNK)

    rp = PROMPT_BLOCK * BLOCKS_PER_STEP
    rs = dec_batch * dec_seq
    steps_per_stream = seq // rp
    prompt_steps = batch * steps_per_stream
    row = lambda a: a.reshape(1, -1)

    sample_step = prompt_steps // 2

    def x_map(step):
        blk = jnp.where(step > sample_step, step - 1, step)
        return blk // steps_per_stream, blk % steps_per_stream, 0

    def y_map(step):
        blk = jnp.where(step >= sample_step, step - 1, step)
        return blk // steps_per_stream, blk % steps_per_stream, 0

    def state_map(step):
        return y_map(step)[0], 0, 0, 0

    any_space = pl.BlockSpec(memory_space=pl.ANY)

    y_prompt, state_p, y_sample, state_s, v_sample = pl.pallas_call(
        functools.partial(_fused_kernel, sample_step=sample_step, last_step=prompt_steps,
                          steps_per_stream=steps_per_stream, n_streams=dec_batch, frames=dec_seq),
        grid=(prompt_steps + 1,),
        in_specs=[pl.BlockSpec((1, rp, D_MODEL), x_map), any_space, any_space,
                  _full((1, D_MODEL)), any_space, _full(lb_logits.shape),
                  _full((1, D_A)), _full((1, D_B)), _full((1, D_B)),
                  _full((H_B, MLP_CHUNK, MLP_CHUNK)), _full((H_B, MLP_CHUNK)),
                  any_space, _full((1, D_MODEL))],
        out_specs=[pl.BlockSpec((1, rp, D_MODEL), y_map),
                   pl.BlockSpec((1, H_A, DK, DV), state_map),
                   any_space, any_space, any_space],
        out_shape=[jax.ShapeDtypeStruct((batch, seq, D_MODEL), _F32),
                   jax.ShapeDtypeStruct((batch, H_A, DK, DV), _F32),
                   jax.ShapeDtypeStruct((rs, D_MODEL), _F32),
                   jax.ShapeDtypeStruct((dec_batch, H_A, DK, DV), _F32),
                   jax.ShapeDtypeStruct((rs, D_B), _F32)],
        scratch_shapes=[pltpu.VMEM((1, H_A, DV, DK), _F32),
                        pltpu.VMEM((1, H_A, DV, DK), _F32),
                        pltpu.VMEM((D_MODEL, IN_WIDTH), _BF16),
                        pltpu.VMEM((D_MODEL, D_MODEL), _BF16),
                        pltpu.VMEM((WEIGHT_STAGE_SLABS, D_MODEL, D_A), _F32),
                        pltpu.SemaphoreType.DMA((WEIGHT_STAGE_SLABS * (D_A // WEIGHT_STAGE_COLS),)),
                        pltpu.VMEM((rs, D_MODEL), _F32),
                        pltpu.VMEM((dec_batch, H_A, DK, DV), _F32),
                        pltpu.VMEM((rs, D_MODEL), _F32),
                        pltpu.VMEM((dec_batch, H_A, DK, DV), _F32),
                        pltpu.VMEM((rs, D_B), _F32),
                        pltpu.SemaphoreType.DMA((5,))],
        compiler_params=pltpu.CompilerParams(
            dimension_semantics=("arbitrary",), vmem_limit_bytes=VMEM_LIMIT_BYTES),
        name="layer_step",
    )(x_prompt, x_sample.reshape(rs, D_MODEL), state_hgrn[0],
      row(norm_w[0]), w_in.reshape(D_MODEL, IN_WIDTH), lb_logits, row(g_norm_w[0]), row(ln_v_w[0]), row(ln_v_b[0]),
      w_s[0], b_s[0], w_out.reshape(D_MODEL, D_MODEL), row(final_norm_w))

    return (y_prompt,
            y_sample.reshape(dec_batch, dec_seq, D_MODEL),
            state_p[None],
            state_s[None],
            v_sample.reshape(1, dec_batch, dec_seq, D_B))
```

```python
import functools
import math

import jax
import jax.numpy as jnp
from jax import lax
from jax.experimental import pallas as pl
from jax.experimental.pallas import tpu as pltpu

D_MODEL = 1024
D_A = 512
D_B = 512
H_A = 4
DK = 128
DV = 128
H_B = 4
C_B = 128
MLP_CHUNK = 128
HGRN_CHUNK = 64
IN_WIDTH = 4 * D_A + 3 * D_B
EPS = 1e-6

PROMPT_BLOCK = 512
BLOCKS_PER_STEP = 2
PROMPT_GROUP = 128
PROMPT_CHUNK = 128
SUBLANES = 8
LOG_DECAY_LIMIT = 78.0
VMEM_LIMIT_BYTES = 60 * 1024 * 1024
PROJECTION_PLAN = {0: (1, 0), 1: (2, 4), 3: (5,), 5: (3, 6)}

WEIGHT_STAGE_COLS = 256
WEIGHT_STAGE_SLABS = 3
OUT_PARTS = 4

_NT = (((1,), (1,)), ((), ()))
_TN = (((0,), (0,)), ((), ()))
_F32 = jnp.float32
_BF16 = jnp.bfloat16


def _dot(a, b, dims=None):
    if dims is None:
        return jnp.dot(a, b, preferred_element_type=_F32)
    return lax.dot_general(a, b, dims, preferred_element_type=_F32)


def _rms(x, w):
    return x * lax.rsqrt(jnp.mean(x * x, axis=-1, keepdims=True) + EPS) * w


def _half_tanh_half(x):
    h = 0.5 * x
    return jnp.tanh(h), h


def _silu(x):
    t, h = _half_tanh_half(x)
    return h * t + h


def _gelu_exact(x):
    return 0.5 * x * (1.0 + lax.erf(x * math.sqrt(0.5)))


def _same_chunk_causal(rows, chunk):
    shift = chunk.bit_length() - 1
    assert chunk == 1 << shift
    t = lax.broadcasted_iota(jnp.int32, (rows, rows), 0)
    s = lax.broadcasted_iota(jnp.int32, (rows, rows), 1)
    same = lax.shift_right_logical(t, shift) == lax.shift_right_logical(s, shift)
    return jnp.logical_and(same, s <= t)


def _lane_cat(a, group):
    n = a.shape[0] // group
    if n == 1:
        return a
    return jnp.concatenate([a[gi * group:(gi + 1) * group] for gi in range(n)], axis=1)


def _lane_split(a, n):
    if n == 1:
        return a
    w = a.shape[1] // n
    return jnp.concatenate([a[:, gi * w:(gi + 1) * w] for gi in range(n)], axis=0)


def _chunk_cumsum(g, chunk):
    rows, width = g.shape
    n_regs = rows // SUBLANES
    x = g.reshape(n_regs, SUBLANES, width)
    sub = lax.broadcasted_iota(jnp.int32, x.shape, 1)
    shift = 1
    while shift < SUBLANES:
        x = x + jnp.where(sub >= shift, pltpu.roll(x, shift, 1), 0.0)
        shift *= 2
    regs_per_chunk = chunk // SUBLANES
    if regs_per_chunk > 1:
        parts, run = [], None
        for r in range(n_regs):
            part = x[r] + run if r % regs_per_chunk else x[r]
            run = part[SUBLANES - 1:SUBLANES, :]
            parts.append(part)
        return jnp.concatenate(parts, axis=0)
    return x.reshape(rows, width)


def _recurrence_fast(qs, kk, g, iv, side, *, states_t, group, chunk, carry_state):
    rows = qs.shape[0]
    n_groups = rows // group
    chunks_per_group = group // chunk
    n_chunks = rows // chunk
    blk_mask = _same_chunk_causal(group, chunk)
    b = _chunk_cumsum(g, chunk)
    side(1)

    iv = iv().astype(_BF16)
    q_dec = (qs * jnp.exp(b)).astype(_BF16)
    k_inv = kk * jnp.exp(-b)
    k_inv_bf = k_inv.astype(_BF16)

    low = None
    for c in range(n_chunks):
        b_last = b[(c + 1) * chunk - 1:(c + 1) * chunk, :]
        low = b_last if low is None else jnp.minimum(low, b_last)

    new_states = [[None] * H_A for _ in range(n_chunks)]
    o_heads = []
    groups = [slice(gi * group, (gi + 1) * group) for gi in range(n_groups)]
    chunks = [slice(c * chunk, (c + 1) * chunk) for c in range(n_chunks)]

    def first_stage(hd):
        ln = slice(hd * DK, (hd + 1) * DK)
        scores = [jnp.where(blk_mask, _dot(q_dec[rg, ln], k_inv_bf[rg, ln], _NT), 0.0).astype(_BF16)
                  for rg in groups]
        dec_last, grow = [], []
        for c, rw in enumerate(chunks):
            last = (c + 1) * chunk - 1
            dec_last.append(jnp.exp(b[last:last + 1, ln]))
            k_upd = (k_inv[rw, ln] * dec_last[c]).astype(_BF16)
            grow.append(_dot(iv[rw, ln], k_upd, _TN))
        return scores, dec_last, grow

    staged = {0: first_stage(0)}
    for hd in range(H_A):
        ln = slice(hd * DK, (hd + 1) * DK)
        if hd + 1 < H_A:
            staged[hd + 1] = first_stage(hd + 1)
        scores, dec_last, grow = staged.pop(hd)
        intra = [_dot(sc, iv[rg, ln]) for sc, rg in zip(scores, groups)]
        o_chunks = []
        s_t = states_t[0][hd]
        for c, rw in enumerate(chunks):
            if not carry_state:
                s_t = states_t[c][hd]
            inter = _dot(q_dec[rw, ln], s_t.astype(_BF16), _NT)
            gi, ci = divmod(c, chunks_per_group)
            o_chunks.append(inter + intra[gi][ci * chunk:(ci + 1) * chunk])
            s_t = s_t * dec_last[c] + grow[c]
            new_states[c][hd] = s_t
        o_heads.append(jnp.concatenate(o_chunks, axis=0) if n_chunks > 1 else o_chunks[0])
        side(2 + hd)
    return jnp.concatenate(o_heads, axis=1), (new_states, jnp.min(low))


def _recurrence_exact(qs, kk, g, iv, side, *, work_ref, state_ref, group, chunk):
    rows = qs.shape[0]
    assert chunk % SUBLANES == 0 and group % SUBLANES == 0
    minis_per_chunk = chunk // SUBLANES
    b = _chunk_cumsum(g, SUBLANES)
    half = work_ref.shape[1] // 2
    assert rows <= half and work_ref.shape[0] * 2 >= 5

    def staged(k, r0, n):
        return work_ref.at[k // 2, pl.ds((k % 2) * half + r0, n), :]

    for k, a in enumerate((qs, kk, b, iv())):
        staged(k, 0, rows)[...] = a
    t_idx = lax.broadcasted_iota(jnp.int32, (SUBLANES, DK), 0)

    def mini(m, carry):
        r0 = pl.multiple_of(m * SUBLANES, SUBLANES)
        qs8, kk8, b8, iv8 = (staged(k, r0, SUBLANES)[...] for k in range(4))
        c = m // minis_per_chunk
        for hd in range(H_A):
            ln = slice(hd * DK, (hd + 1) * DK)
            q_h, k_h, b_h, v_h = qs8[:, ln], kk8[:, ln], b8[:, ln], iv8[:, ln]
            s_t = state_ref[c, hd]
            acc = _dot(q_h * jnp.exp(b_h), s_t, _NT)
            for s in range(SUBLANES):
                dec = jnp.exp(jnp.minimum(b_h - b_h[s:s + 1, :], 0.0))
                term = jnp.where(t_idx >= s, q_h * k_h[s:s + 1, :] * dec, 0.0)
                acc = acc + jnp.sum(term, axis=1, keepdims=True) * v_h[s:s + 1, :]
            staged(4, r0, SUBLANES)[:, ln] = acc
            b_end = b_h[SUBLANES - 1:SUBLANES, :]
            k_upd = k_h * jnp.exp(b_end - b_h)
            state_ref[c, hd] = s_t * jnp.exp(b_end) + _dot(v_h, k_upd, _TN)
        return carry

    lax.fori_loop(0, rows // SUBLANES, mini, 0)
    return staged(4, 0, rows)[...], None


def _layer_block(x, refs, recurrence, *, group, mix_chunk):
    (norm_w_ref, w_in_ref, lb_logits_ref, g_norm_w_ref, ln_v_w_ref, ln_v_b_ref,
     w_s_ref, b_s_ref, w_out_ref, final_w_ref) = refs
    n_groups = x.shape[0] // group

    h = _rms(x, norm_w_ref[...]).astype(_BF16)
    slices = {}

    def proj(k):
        if k not in slices:
            slices[k] = _dot(h, w_in_ref[:, k * D_A:(k + 1) * D_A])
        return slices[k]

    def side(point):
        for k in PROJECTION_PLAN.get(point, ()):
            proj(k)

    logits = lb_logits_ref[...]
    e = jnp.exp(logits - jnp.max(logits, axis=0, keepdims=True))
    lb = e[0:1, :] / jnp.sum(e, axis=0, keepdims=True)

    side(0)
    half_span = 0.5 * (1.0 - lb)
    f = (lb + half_span) + half_span * _half_tanh_half(proj(1))[0]
    qs = _silu(proj(0)) * (DK ** -0.5)
    o, aux = recurrence(qs, 1.0 - f, jnp.log(f), lambda: proj(2), side)

    ga = proj(3)
    y_a_heads = []
    for hd in range(H_A):
        o_h = o[:, hd * DV:(hd + 1) * DV]
        y_a_heads.append(o_h * lax.rsqrt(jnp.mean(o_h * o_h, axis=-1, keepdims=True) + EPS))
    y_a = jnp.concatenate(y_a_heads, axis=1) * g_norm_w_ref[...] * _silu(ga)

    u = _gelu_exact(proj(4))
    v = _gelu_exact(proj(5))
    mu = jnp.mean(v, axis=-1, keepdims=True)
    vc = v - mu
    var = jnp.mean(vc * vc, axis=-1, keepdims=True)
    v_norm = vc * lax.rsqrt(var + EPS) * ln_v_w_ref[...] + ln_v_b_ref[...]
    gb = proj(6)
    v_bf = v_norm.astype(_BF16)
    mix_mask = _same_chunk_causal(group, mix_chunk)
    pos_r = lax.broadcasted_iota(jnp.int32, (group, MLP_CHUNK), 0)
    pos_c = lax.broadcasted_iota(jnp.int32, (group, MLP_CHUNK), 1)
    pos_mask = jnp.bitwise_and(pos_r, mix_chunk - 1) == pos_c
    mix_heads = []
    for hb in range(H_B):
        ln = slice(hb * C_B, (hb + 1) * C_B)
        if group == mix_chunk == MLP_CHUNK:
            w_full = w_s_ref[hb]
        else:
            w_rows = jnp.concatenate([w_s_ref[hb][:mix_chunk]] * (group // mix_chunk), axis=0)
            w_full = _dot(w_rows.astype(_BF16), jnp.where(pos_mask, 1.0, 0.0).astype(_BF16), _NT)
        w = jnp.where(mix_mask, w_full, 0.0).astype(_BF16)
        bias = jnp.sum(jnp.where(pos_mask, b_s_ref[hb:hb + 1, :], 0.0), axis=1, keepdims=True)
        mixed = _dot(w, _lane_cat(v_bf[:, ln], group)) + bias
        mix_heads.append(_lane_split(mixed, n_groups))
    mix = jnp.concatenate(mix_heads, axis=1)
    y_b = u * mix * _silu(gb)

    y = jnp.concatenate([y_a, y_b], axis=1).astype(_BF16)
    part = max(x.shape[0] // OUT_PARTS, group)
    outs = []
    for r0 in range(0, x.shape[0], part):
        out = x[r0:r0 + part] + _dot(y[r0:r0 + part], w_out_ref[...])
        outs.append(_rms(out, final_w_ref[...]))
    return jnp.concatenate(outs, axis=0), v_norm, aux


def _stage_weights_bf16(pairs, stage_ref, sem_ref):
    chunks = [(src, dst, c0) for src, dst in pairs for c0 in range(0, src.shape[1], WEIGHT_STAGE_COLS)]
    per_slab = stage_ref.shape[2] // WEIGHT_STAGE_COLS
    slots = stage_ref.shape[0] * per_slab
    ahead = slots - 1

    def slot(i):
        s = i % slots
        return stage_ref.at[s // per_slab, :, pl.ds((s % per_slab) * WEIGHT_STAGE_COLS, WEIGHT_STAGE_COLS)]

    def copy(i):
        src, _, c0 = chunks[i]
        return pltpu.make_async_copy(src.at[:, c0:c0 + WEIGHT_STAGE_COLS], slot(i), sem_ref.at[i % slots])

    for i in range(min(ahead, len(chunks))):
        copy(i).start()
    for i, (_, dst, c0) in enumerate(chunks):
        if i + ahead < len(chunks):
            copy(i + ahead).start()
        copy(i).wait()
        dst[:, c0:c0 + WEIGHT_STAGE_COLS] = slot(i)[...].astype(_BF16)


def _fused_kernel(xp_ref, xs_hbm, state_in_hbm, norm_w_ref, w_in_hbm, lb_logits_ref, g_norm_w_ref,
                  ln_v_w_ref, ln_v_b_ref, w_s_ref, b_s_ref, w_out_hbm, final_w_ref,
                  yp_ref, state_p_ref, ys_hbm, state_s_hbm, v_hbm,
                  state_t_scr, state_prev_scr, w_in_ref, w_out_ref, stage_scr, stage_sem,
                  xs_ref, state_in_ref, ys_ref, state_s_ref, v_ref, io_sem,
                  *, sample_step, last_step, steps_per_stream, n_streams, frames):
    step = pl.program_id(0)
    refs = (norm_w_ref, w_in_ref, lb_logits_ref, g_norm_w_ref, ln_v_w_ref, ln_v_b_ref,
            w_s_ref, b_s_ref, w_out_ref, final_w_ref)
    sample_in = [pltpu.make_async_copy(src, dst, io_sem.at[i])
                 for i, (src, dst) in enumerate([(xs_hbm, xs_ref), (state_in_hbm, state_in_ref)])]
    sample_out = [pltpu.make_async_copy(src, dst, io_sem.at[len(sample_in) + i])
                  for i, (src, dst) in enumerate([(ys_ref, ys_hbm), (state_s_ref, state_s_hbm), (v_ref, v_hbm)])]

    @pl.when(step == 0)
    def _():
        _stage_weights_bf16([(w_in_hbm, w_in_ref), (w_out_hbm, w_out_ref)], stage_scr, stage_sem)
        for copy in sample_in:
            copy.start()

    def too_steep(low):
        return jnp.logical_not(low >= -LOG_DECAY_LIMIT)

    @pl.when(step != sample_step)
    def _prompt():
        j = lax.rem(jnp.where(step > sample_step, step - 1, step), steps_per_stream)

        @pl.when(j == 0)
        def _():
            state_t_scr[...] = jnp.zeros_like(state_t_scr)

        state_prev_scr[...] = state_t_scr[...]
        states = [state_t_scr[0, hd] for hd in range(H_A)]
        low = None
        for blk in range(BLOCKS_PER_STEP):
            rows = slice(blk * PROMPT_BLOCK, (blk + 1) * PROMPT_BLOCK)
            fast = functools.partial(_recurrence_fast, states_t=[states], group=PROMPT_GROUP,
                                     chunk=PROMPT_CHUNK, carry_state=True)
            y, _, (new_states, low_blk) = _layer_block(xp_ref[0, rows, :], refs, fast,
                                                       group=PROMPT_GROUP, mix_chunk=MLP_CHUNK)
            yp_ref[0, rows, :] = y
            states = new_states[-1]
            low = low_blk if low is None else jnp.minimum(low, low_blk)
        for hd in range(H_A):
            state_t_scr[0, hd] = states[hd]

        @pl.when(too_steep(low))
        def _():
            state_t_scr[...] = state_prev_scr[...]
            exact = functools.partial(_recurrence_exact, work_ref=stage_scr, state_ref=state_t_scr,
                                      group=PROMPT_GROUP, chunk=PROMPT_BLOCK)
            for blk in range(BLOCKS_PER_STEP):
                rows = slice(blk * PROMPT_BLOCK, (blk + 1) * PROMPT_BLOCK)
                y_exact, _, _ = _layer_block(xp_ref[0, rows, :], refs, exact,
                                             group=PROMPT_GROUP, mix_chunk=MLP_CHUNK)
                yp_ref[0, rows, :] = y_exact

        @pl.when(j == steps_per_stream - 1)
        def _():
            for hd in range(H_A):
                state_p_ref[0, hd] = state_t_scr[0, hd].T

    @pl.when(step == sample_step)
    def _sample():
        for copy in sample_in:
            copy.wait()
        rows = n_streams * frames
        fast = functools.partial(
            _recurrence_fast,
            states_t=[[state_in_ref[s, hd].T for hd in range(H_A)] for s in range(n_streams)],
            group=rows, chunk=frames, carry_state=False)
        y, v_norm, (new_states, low) = _layer_block(xs_ref[...], refs, fast, group=rows, mix_chunk=frames)
        ys_ref[...] = y
        v_ref[...] = v_norm
        for s in range(n_streams):
            for hd in range(H_A):
                state_s_ref[s, hd] = new_states[s][hd].T

        @pl.when(too_steep(low))
        def _():
            for s in range(n_streams):
                for hd in range(H_A):
                    state_s_ref[s, hd] = state_in_ref[s, hd].T
            exact = functools.partial(_recurrence_exact, work_ref=stage_scr, state_ref=state_s_ref,
                                      group=rows, chunk=frames)
            y_exact, v_exact, _ = _layer_block(xs_ref[...], refs, exact, group=rows, mix_chunk=frames)
            ys_ref[...] = y_exact
            v_ref[...] = v_exact
            for s in range(n_streams):
                for hd in range(H_A):
                    state_s_ref[s, hd] = state_s_ref[s, hd].T

        for copy in sample_out:
            copy.start()

    @pl.when(step == last_step)
    def _():
        for copy in sample_out:
            copy.wait()


def _full(shape):
    return pl.BlockSpec(shape, lambda *_: (0,) * len(shape))


def kernel(x_prompt, x_sample, state_hgrn, norm_w, w_in, lb_logits, g_norm_w, ln_v_w, ln_v_b, w_s, b_s, w_out, final_norm_w):
    batch, seq, _ = x_prompt.shape
    dec_batch, dec_seq, _ = x_sample.shape
    assert norm_w.shape[0] == 1, "single-layer trunk"
    assert seq % (PROMPT_BLOCK * BLOCKS_PER_STEP) == 0 and PROMPT_BLOCK % PROMPT_GROUP == 0
    assert PROMPT_GROUP % MLP_CHUNK == 0 and MLP_CHUNK % PROMPT_CHUNK == 0
    assert dec_seq <= HGRN_CHUNK and dec_seq <= MLP_CHUNK and dec_seq & (dec_seq - 1) == 0
    assert w_s.shape[2:] == (MLP_CHUNK, MLP_CHUNK)

    rp = PROMPT_BLOCK * BLOCKS_PER_STEP
    rs = dec_batch * dec_seq
    steps_per_stream = seq // rp
    prompt_steps = batch * steps_per_stream
    row = lambda a: a.reshape(1, -1)

    sample_step = prompt_steps // 2

    def x_map(step):
        blk = jnp.where(step > sample_step, step - 1, step)
        return blk // steps_per_stream, blk % steps_per_stream, 0

    def y_map(step):
        blk = jnp.where(step >= sample_step, step - 1, step)
        return blk // steps_per_stream, blk % steps_per_stream, 0

    def state_map(step):
        return y_map(step)[0], 0, 0, 0

    any_space = pl.BlockSpec(memory_space=pl.ANY)

    y_prompt, state_p, y_sample, state_s, v_sample = pl.pallas_call(
        functools.partial(_fused_kernel, sample_step=sample_step, last_step=prompt_steps,
                          steps_per_stream=steps_per_stream, n_streams=dec_batch, frames=dec_seq),
        grid=(prompt_steps + 1,),
        in_specs=[pl.BlockSpec((1, rp, D_MODEL), x_map), any_space, any_space,
                  _full((1, D_MODEL)), any_space, _full(lb_logits.shape),
                  _full((1, D_A)), _full((1, D_B)), _full((1, D_B)),
                  _full((H_B, MLP_CHUNK, MLP_CHUNK)), _full((H_B, MLP_CHUNK)),
                  any_space, _full((1, D_MODEL))],
        out_specs=[pl.BlockSpec((1, rp, D_MODEL), y_map),
                   pl.BlockSpec((1, H_A, DK, DV), state_map),
                   any_space, any_space, any_space],
        out_shape=[jax.ShapeDtypeStruct((batch, seq, D_MODEL), _F32),
                   jax.ShapeDtypeStruct((batch, H_A, DK, DV), _F32),
                   jax.ShapeDtypeStruct((rs, D_MODEL), _F32),
                   jax.ShapeDtypeStruct((dec_batch, H_A, DK, DV), _F32),
                   jax.ShapeDtypeStruct((rs, D_B), _F32)],
        scratch_shapes=[pltpu.VMEM((1, H_A, DV, DK), _F32),
                        pltpu.VMEM((1, H_A, DV, DK), _F32),
                        pltpu.VMEM((D_MODEL, IN_WIDTH), _BF16),
                        pltpu.VMEM((D_MODEL, D_MODEL), _BF16),
                        pltpu.VMEM((WEIGHT_STAGE_SLABS, D_MODEL, D_A), _F32),
                        pltpu.SemaphoreType.DMA((WEIGHT_STAGE_SLABS * (D_A // WEIGHT_STAGE_COLS),)),
                        pltpu.VMEM((rs, D_MODEL), _F32),
                        pltpu.VMEM((dec_batch, H_A, DK, DV), _F32),
                        pltpu.VMEM((rs, D_MODEL), _F32),
                        pltpu.VMEM((dec_batch, H_A, DK, DV), _F32),
                        pltpu.VMEM((rs, D_B), _F32),
                        pltpu.SemaphoreType.DMA((5,))],
        compiler_params=pltpu.CompilerParams(
            dimension_semantics=("arbitrary",), vmem_limit_bytes=VMEM_LIMIT_BYTES),
        name="layer_step",
    )(x_prompt, x_sample.reshape(rs, D_MODEL), state_hgrn[0],
      row(norm_w[0]), w_in.reshape(D_MODEL, IN_WIDTH), lb_logits, row(g_norm_w[0]), row(ln_v_w[0]), row(ln_v_b[0]),
      w_s[0], b_s[0], w_out.reshape(D_MODEL, D_MODEL), row(final_norm_w))

    return (y_prompt,
            y_sample.reshape(dec_batch, dec_seq, D_MODEL),
            state_p[None],
            state_s[None],
            v_sample.reshape(1, dec_batch, dec_seq, D_B))
```

```python
import functools
import math

import jax
import jax.numpy as jnp
from jax import lax
from jax.experimental import pallas as pl
from jax.experimental.pallas import tpu as pltpu

D_MODEL = 1024
D_A = 512
D_B = 512
H_A = 4
DK = 128
DV = 128
H_B = 4
C_B = 128
MLP_CHUNK = 128
HGRN_CHUNK = 64
IN_WIDTH = 4 * D_A + 3 * D_B
EPS = 1e-6

PROMPT_BLOCK = 512
BLOCKS_PER_STEP = 2
PROMPT_GROUP = 128
PROMPT_CHUNK = 128
SUBLANES = 8
LOG_DECAY_LIMIT = 78.0
VMEM_LIMIT_BYTES = 60 * 1024 * 1024
PROJECTION_PLAN = {0: (1, 0), 1: (2, 4), 3: (5,), 5: (3, 6)}

WEIGHT_STAGE_COLS = 256
WEIGHT_STAGE_SLABS = 3
OUT_PARTS = 4

_NT = (((1,), (1,)), ((), ()))
_TN = (((0,), (0,)), ((), ()))
_F32 = jnp.float32
_BF16 = jnp.bfloat16


def _dot(a, b, dims=None):
    if dims is None:
        return jnp.dot(a, b, preferred_element_type=_F32)
    return lax.dot_general(a, b, dims, preferred_element_type=_F32)


def _rms(x, w):
    return x * lax.rsqrt(jnp.mean(x * x, axis=-1, keepdims=True) + EPS) * w


def _half_tanh_half(x):
    h = 0.5 * x
    return jnp.tanh(h), h


def _silu(x):
    t, h = _half_tanh_half(x)
    return h * t + h


def _gelu_exact(x):
    return 0.5 * x * (1.0 + lax.erf(x * math.sqrt(0.5)))


def _same_chunk_causal(rows, chunk):
    shift = chunk.bit_length() - 1
    assert chunk == 1 << shift
    t = lax.broadcasted_iota(jnp.int32, (rows, rows), 0)
    s = lax.broadcasted_iota(jnp.int32, (rows, rows), 1)
    same = lax.shift_right_logical(t, shift) == lax.shift_right_logical(s, shift)
    return jnp.logical_and(same, s <= t)


def _lane_cat(a, group):
    n = a.shape[0] // group
    if n == 1:
        return a
    return jnp.concatenate([a[gi * group:(gi + 1) * group] for gi in range(n)], axis=1)


def _lane_split(a, n):
    if n == 1:
        return a
    w = a.shape[1] // n
    return jnp.concatenate([a[:, gi * w:(gi + 1) * w] for gi in range(n)], axis=0)


def _chunk_cumsum(g, chunk):
    rows, width = g.shape
    n_regs = rows // SUBLANES
    x = g.reshape(n_regs, SUBLANES, width)
    sub = lax.broadcasted_iota(jnp.int32, x.shape, 1)
    shift = 1
    while shift < SUBLANES:
        x = x + jnp.where(sub >= shift, pltpu.roll(x, shift, 1), 0.0)
        shift *= 2
    regs_per_chunk = chunk // SUBLANES
    if regs_per_chunk > 1:
        parts, run = [], None
        for r in range(n_regs):
            part = x[r] + run if r % regs_per_chunk else x[r]
            run = part[SUBLANES - 1:SUBLANES, :]
            parts.append(part)
        return jnp.concatenate(parts, axis=0)
    return x.reshape(rows, width)


def _recurrence_fast(qs, kk, g, iv, side, *, states_t, group, chunk, carry_state):
    rows = qs.shape[0]
    n_groups = rows // group
    chunks_per_group = group // chunk
    n_chunks = rows // chunk
    blk_mask = _same_chunk_causal(group, chunk)
    b = _chunk_cumsum(g, chunk)
    side(1)

    iv = iv().astype(_BF16)
    q_dec = (qs * jnp.exp(b)).astype(_BF16)
    k_inv = kk * jnp.exp(-b)
    k_inv_bf = k_inv.astype(_BF16)

    low = None
    for c in range(n_chunks):
        b_last = b[(c + 1) * chunk - 1:(c + 1) * chunk, :]
        low = b_last if low is None else jnp.minimum(low, b_last)

    if not carry_state:
        ends = [b[(c + 1) * chunk - 1:(c + 1) * chunk, :] for c in range(n_chunks)]
        ends.append(jnp.zeros((DK - n_chunks, b.shape[1]), _F32))
        dec_cols = jnp.exp(jnp.concatenate(ends, axis=0)).T

    new_states = [[None] * H_A for _ in range(n_chunks)]
    o_heads = []
    for hd in range(H_A):
        ln = slice(hd * DK, (hd + 1) * DK)
        groups = [slice(gi * group, (gi + 1) * group) for gi in range(n_groups)]
        chunks = [slice(c * chunk, (c + 1) * chunk) for c in range(n_chunks)]
        scores = [jnp.where(blk_mask, _dot(q_dec[rg, ln], k_inv_bf[rg, ln], _NT), 0.0).astype(_BF16)
                  for rg in groups]
        dec_last, grow = [], []
        for c, rw in enumerate(chunks):
            last = (c + 1) * chunk - 1
            dec_last.append(jnp.exp(b[last:last + 1, ln]))
            k_upd = (k_inv[rw, ln] * dec_last[c]).astype(_BF16)
            if carry_state:
                grow.append(_dot(iv[rw, ln], k_upd, _TN))
            else:
                grow.append(_dot(k_upd, iv[rw, ln], _TN))
        intra = [_dot(sc, iv[rg, ln]) for sc, rg in zip(scores, groups)]
        o_chunks = []
        s_t = states_t[0][hd]
        for c, rw in enumerate(chunks):
            if carry_state:
                inter = _dot(q_dec[rw, ln], s_t.astype(_BF16), _NT)
                s_t = s_t * dec_last[c] + grow[c]
                new_states[c][hd] = s_t
            else:
                s = states_t[c][hd]
                inter = _dot(q_dec[rw, ln], s.astype(_BF16))
                new_states[c][hd] = s * dec_cols[ln, c:c + 1] + grow[c]
            gi, ci = divmod(c, chunks_per_group)
            o_chunks.append(inter + intra[gi][ci * chunk:(ci + 1) * chunk])
        o_heads.append(jnp.concatenate(o_chunks, axis=0) if n_chunks > 1 else o_chunks[0])
        side(2 + hd)
    return jnp.concatenate(o_heads, axis=1), (new_states, jnp.min(low))


def _recurrence_exact(qs, kk, g, iv, side, *, work_ref, state_ref, group, chunk):
    rows = qs.shape[0]
    assert chunk % SUBLANES == 0 and group % SUBLANES == 0
    minis_per_chunk = chunk // SUBLANES
    b = _chunk_cumsum(g, SUBLANES)
    half = work_ref.shape[1] // 2
    assert rows <= half and work_ref.shape[0] * 2 >= 5

    def staged(k, r0, n):
        return work_ref.at[k // 2, pl.ds((k % 2) * half + r0, n), :]

    for k, a in enumerate((qs, kk, b, iv())):
        staged(k, 0, rows)[...] = a
    t_idx = lax.broadcasted_iota(jnp.int32, (SUBLANES, DK), 0)

    def mini(m, carry):
        r0 = pl.multiple_of(m * SUBLANES, SUBLANES)
        qs8, kk8, b8, iv8 = (staged(k, r0, SUBLANES)[...] for k in range(4))
        c = m // minis_per_chunk
        for hd in range(H_A):
            ln = slice(hd * DK, (hd + 1) * DK)
            q_h, k_h, b_h, v_h = qs8[:, ln], kk8[:, ln], b8[:, ln], iv8[:, ln]
            s_t = state_ref[c, hd]
            acc = _dot(q_h * jnp.exp(b_h), s_t, _NT)
            for s in range(SUBLANES):
                dec = jnp.exp(jnp.minimum(b_h - b_h[s:s + 1, :], 0.0))
                term = jnp.where(t_idx >= s, q_h * k_h[s:s + 1, :] * dec, 0.0)
                acc = acc + jnp.sum(term, axis=1, keepdims=True) * v_h[s:s + 1, :]
            staged(4, r0, SUBLANES)[:, ln] = acc
            b_end = b_h[SUBLANES - 1:SUBLANES, :]
            k_upd = k_h * jnp.exp(b_end - b_h)
            state_ref[c, hd] = s_t * jnp.exp(b_end) + _dot(v_h, k_upd, _TN)
        return carry

    lax.fori_loop(0, rows // SUBLANES, mini, 0)
    return staged(4, 0, rows)[...], None


def _layer_block(x, refs, recurrence, *, group, mix_chunk):
    (norm_w_ref, w_in_ref, lb_logits_ref, g_norm_w_ref, ln_v_w_ref, ln_v_b_ref,
     w_s_ref, b_s_ref, w_out_ref, final_w_ref) = refs
    n_groups = x.shape[0] // group

    h = _rms(x, norm_w_ref[...]).astype(_BF16)
    slices = {}

    def proj(k):
        if k not in slices:
            slices[k] = _dot(h, w_in_ref[:, k * D_A:(k + 1) * D_A])
        return slices[k]

    def side(point):
        for k in PROJECTION_PLAN.get(point, ()):
            proj(k)

    logits = lb_logits_ref[...]
    e = jnp.exp(logits - jnp.max(logits, axis=0, keepdims=True))
    lb = e[0:1, :] / jnp.sum(e, axis=0, keepdims=True)

    side(0)
    half_span = 0.5 * (1.0 - lb)
    f = (lb + half_span) + half_span * _half_tanh_half(proj(1))[0]
    qs = _silu(proj(0)) * (DK ** -0.5)
    o, aux = recurrence(qs, 1.0 - f, jnp.log(f), lambda: proj(2), side)

    ga = proj(3)
    y_a_heads = []
    for hd in range(H_A):
        o_h = o[:, hd * DV:(hd + 1) * DV]
        y_a_heads.append(o_h * lax.rsqrt(jnp.mean(o_h * o_h, axis=-1, keepdims=True) + EPS))
    y_a = jnp.concatenate(y_a_heads, axis=1) * g_norm_w_ref[...] * _silu(ga)

    u = _gelu_exact(proj(4))
    v = _gelu_exact(proj(5))
    mu = jnp.mean(v, axis=-1, keepdims=True)
    vc = v - mu
    var = jnp.mean(vc * vc, axis=-1, keepdims=True)
    v_norm = vc * lax.rsqrt(var + EPS) * ln_v_w_ref[...] + ln_v_b_ref[...]
    gb = proj(6)
    v_bf = v_norm.astype(_BF16)
    mix_mask = _same_chunk_causal(group, mix_chunk)
    pos_r = lax.broadcasted_iota(jnp.int32, (group, MLP_CHUNK), 0)
    pos_c = lax.broadcasted_iota(jnp.int32, (group, MLP_CHUNK), 1)
    pos_mask = jnp.bitwise_and(pos_r, mix_chunk - 1) == pos_c
    mix_heads = []
    for hb in range(H_B):
        ln = slice(hb * C_B, (hb + 1) * C_B)
        if group == mix_chunk == MLP_CHUNK:
            w_full = w_s_ref[hb]
        else:
            w_rows = jnp.concatenate([w_s_ref[hb][:mix_chunk]] * (group // mix_chunk), axis=0)
            w_full = _dot(w_rows.astype(_BF16), jnp.where(pos_mask, 1.0, 0.0).astype(_BF16), _NT)
        w = jnp.where(mix_mask, w_full, 0.0).astype(_BF16)
        bias = jnp.sum(jnp.where(pos_mask, b_s_ref[hb:hb + 1, :], 0.0), axis=1, keepdims=True)
        mixed = _dot(w, _lane_cat(v_bf[:, ln], group)) + bias
        mix_heads.append(_lane_split(mixed, n_groups))
    mix = jnp.concatenate(mix_heads, axis=1)
    y_b = u * mix * _silu(gb)

    y = jnp.concatenate([y_a, y_b], axis=1).astype(_BF16)
    part = max(x.shape[0] // OUT_PARTS, group)
    outs = []
    for r0 in range(0, x.shape[0], part):
        out = x[r0:r0 + part] + _dot(y[r0:r0 + part], w_out_ref[...])
        outs.append(_rms(out, final_w_ref[...]))
    return jnp.concatenate(outs, axis=0), v_norm, aux


def _stage_weights_bf16(pairs, stage_ref, sem_ref):
    chunks = [(src, dst, c0) for src, dst in pairs for c0 in range(0, src.shape[1], WEIGHT_STAGE_COLS)]
    per_slab = stage_ref.shape[2] // WEIGHT_STAGE_COLS
    slots = stage_ref.shape[0] * per_slab
    ahead = slots - 1

    def slot(i):
        s = i % slots
        return stage_ref.at[s // per_slab, :, pl.ds((s % per_slab) * WEIGHT_STAGE_COLS, WEIGHT_STAGE_COLS)]

    def copy(i):
        src, _, c0 = chunks[i]
        return pltpu.make_async_copy(src.at[:, c0:c0 + WEIGHT_STAGE_COLS], slot(i), sem_ref.at[i % slots])

    for i in range(min(ahead, len(chunks))):
        copy(i).start()
    for i, (_, dst, c0) in enumerate(chunks):
        if i + ahead < len(chunks):
            copy(i + ahead).start()
        copy(i).wait()
        dst[:, c0:c0 + WEIGHT_STAGE_COLS] = slot(i)[...].astype(_BF16)


def _fused_kernel(xp_ref, xs_hbm, state_in_hbm, norm_w_ref, w_in_hbm, lb_logits_ref, g_norm_w_ref,
                  ln_v_w_ref, ln_v_b_ref, w_s_ref, b_s_ref, w_out_hbm, final_w_ref,
                  yp_ref, state_p_ref, ys_hbm, state_s_hbm, v_hbm,
                  state_t_scr, state_prev_scr, w_in_ref, w_out_ref, stage_scr, stage_sem,
                  xs_ref, state_in_ref, ys_ref, state_s_ref, v_ref, io_sem,
                  *, sample_step, last_step, steps_per_stream, n_streams, frames):
    step = pl.program_id(0)
    refs = (norm_w_ref, w_in_ref, lb_logits_ref, g_norm_w_ref, ln_v_w_ref, ln_v_b_ref,
            w_s_ref, b_s_ref, w_out_ref, final_w_ref)
    sample_in = [pltpu.make_async_copy(src, dst, io_sem.at[i])
                 for i, (src, dst) in enumerate([(xs_hbm, xs_ref), (state_in_hbm, state_in_ref)])]
    sample_out = [pltpu.make_async_copy(src, dst, io_sem.at[len(sample_in) + i])
                  for i, (src, dst) in enumerate([(ys_ref, ys_hbm), (state_s_ref, state_s_hbm), (v_ref, v_hbm)])]

    @pl.when(step == 0)
    def _():
        _stage_weights_bf16([(w_in_hbm, w_in_ref), (w_out_hbm, w_out_ref)], stage_scr, stage_sem)
        for copy in sample_in:
            copy.start()

    def too_steep(low):
        return jnp.logical_not(low >= -LOG_DECAY_LIMIT)

    @pl.when(step != sample_step)
    def _prompt():
        j = lax.rem(jnp.where(step > sample_step, step - 1, step), steps_per_stream)

        @pl.when(j == 0)
        def _():
            state_t_scr[...] = jnp.zeros_like(state_t_scr)

        state_prev_scr[...] = state_t_scr[...]
        states = [state_t_scr[0, hd] for hd in range(H_A)]
        low = None
        for blk in range(BLOCKS_PER_STEP):
            rows = slice(blk * PROMPT_BLOCK, (blk + 1) * PROMPT_BLOCK)
            fast = functools.partial(_recurrence_fast, states_t=[states], group=PROMPT_GROUP,
                                     chunk=PROMPT_CHUNK, carry_state=True)
            y, _, (new_states, low_blk) = _layer_block(xp_ref[0, rows, :], refs, fast,
                                                       group=PROMPT_GROUP, mix_chunk=MLP_CHUNK)
            yp_ref[0, rows, :] = y
            states = new_states[-1]
            low = low_blk if low is None else jnp.minimum(low, low_blk)
        for hd in range(H_A):
            state_t_scr[0, hd] = states[hd]

        @pl.when(too_steep(low))
        def _():
            state_t_scr[...] = state_prev_scr[...]
            exact = functools.partial(_recurrence_exact, work_ref=stage_scr, state_ref=state_t_scr,
                                      group=PROMPT_GROUP, chunk=PROMPT_BLOCK)
            for blk in range(BLOCKS_PER_STEP):
                rows = slice(blk * PROMPT_BLOCK, (blk + 1) * PROMPT_BLOCK)
                y_exact, _, _ = _layer_block(xp_ref[0, rows, :], refs, exact,
                                             group=PROMPT_GROUP, mix_chunk=MLP_CHUNK)
                yp_ref[0, rows, :] = y_exact

        @pl.when(j == steps_per_stream - 1)
        def _():
            for hd in range(H_A):
                state_p_ref[0, hd] = state_t_scr[0, hd].T

    @pl.when(step == sample_step)
    def _sample():
        for copy in sample_in:
            copy.wait()
        rows = n_streams * frames
        fast = functools.partial(
            _recurrence_fast,
            states_t=[[state_in_ref[s, hd] for hd in range(H_A)] for s in range(n_streams)],
            group=rows, chunk=frames, carry_state=False)
        y, v_norm, (new_states, low) = _layer_block(xs_ref[...], refs, fast, group=rows, mix_chunk=frames)
        ys_ref[...] = y
        v_ref[...] = v_norm
        for s in range(n_streams):
            for hd in range(H_A):
                state_s_ref[s, hd] = new_states[s][hd]

        @pl.when(too_steep(low))
        def _():
            for s in range(n_streams):
                for hd in range(H_A):
                    state_s_ref[s, hd] = state_in_ref[s, hd].T
            exact = functools.partial(_recurrence_exact, work_ref=stage_scr, state_ref=state_s_ref,
                                      group=rows, chunk=frames)
            y_exact, v_exact, _ = _layer_block(xs_ref[...], refs, exact, group=rows, mix_chunk=frames)
            ys_ref[...] = y_exact
            v_ref[...] = v_exact
            for s in range(n_streams):
                for hd in range(H_A):
                    state_s_ref[s, hd] = state_s_ref[s, hd].T

        for copy in sample_out:
            copy.start()

    @pl.when(step == last_step)
    def _():
        for copy in sample_out:
            copy.wait()


def _full(shape):
    return pl.BlockSpec(shape, lambda *_: (0,) * len(shape))


def kernel(x_prompt, x_sample, state_hgrn, norm_w, w_in, lb_logits, g_norm_w, ln_v_w, ln_v_b, w_s, b_s, w_out, final_norm_w):
    batch, seq, _ = x_prompt.shape
    dec_batch, dec_seq, _ = x_sample.shape
    assert norm_w.shape[0] == 1, "single-layer trunk"
    assert seq % (PROMPT_BLOCK * BLOCKS_PER_STEP) == 0 and PROMPT_BLOCK % PROMPT_GROUP == 0
    assert PROMPT_GROUP % MLP_CHUNK == 0 and MLP_CHUNK % PROMPT_CHUNK == 0
    assert dec_seq <= HGRN_CHUNK and dec_seq <= MLP_CHUNK and dec_seq & (dec_seq - 1) == 0
    assert w_s.shape[2:] == (MLP_CHUNK, MLP_CHUNK)

    rp = PROMPT_BLOCK * BLOCKS_PER_STEP
    rs = dec_batch * dec_seq
    steps_per_stream = seq // rp
    prompt_steps = batch * steps_per_stream
    row = lambda a: a.reshape(1, -1)

    sample_step = prompt_steps // 2

    def x_map(step):
        blk = jnp.where(step > sample_step, step - 1, step)
        return blk // steps_per_stream, blk % steps_per_stream, 0

    def y_map(step):
        blk = jnp.where(step >= sample_step, step - 1, step)
        return blk // steps_per_stream, blk % steps_per_stream, 0

    def state_map(step):
        return y_map(step)[0], 0, 0, 0

    any_space = pl.BlockSpec(memory_space=pl.ANY)

    y_prompt, state_p, y_sample, state_s, v_sample = pl.pallas_call(
        functools.partial(_fused_kernel, sample_step=sample_step, last_step=prompt_steps,
                          steps_per_stream=steps_per_stream, n_streams=dec_batch, frames=dec_seq),
        grid=(prompt_steps + 1,),
        in_specs=[pl.BlockSpec((1, rp, D_MODEL), x_map), any_space, any_space,
                  _full((1, D_MODEL)), any_space, _full(lb_logits.shape),
                  _full((1, D_A)), _full((1, D_B)), _full((1, D_B)),
                  _full((H_B, MLP_CHUNK, MLP_CHUNK)), _full((H_B, MLP_CHUNK)),
                  any_space, _full((1, D_MODEL))],
        out_specs=[pl.BlockSpec((1, rp, D_MODEL), y_map),
                   pl.BlockSpec((1, H_A, DK, DV), state_map),
                   any_space, any_space, any_space],
        out_shape=[jax.ShapeDtypeStruct((batch, seq, D_MODEL), _F32),
                   jax.ShapeDtypeStruct((batch, H_A, DK, DV), _F32),
                   jax.ShapeDtypeStruct((rs, D_MODEL), _F32),
                   jax.ShapeDtypeStruct((dec_batch, H_A, DK, DV), _F32),
                   jax.ShapeDtypeStruct((rs, D_B), _F32)],
        scratch_shapes=[pltpu.VMEM((1, H_A, DV, DK), _F32),
                        pltpu.VMEM((1, H_A, DV, DK), _F32),
                        pltpu.VMEM((D_MODEL, IN_WIDTH), _BF16),
                        pltpu.VMEM((D_MODEL, D_MODEL), _BF16),
                        pltpu.VMEM((WEIGHT_STAGE_SLABS, D_MODEL, D_A), _F32),
                        pltpu.SemaphoreType.DMA((WEIGHT_STAGE_SLABS * (D_A // WEIGHT_STAGE_COLS),)),
                        pltpu.VMEM((rs, D_MODEL), _F32),
                        pltpu.VMEM((dec_batch, H_A, DK, DV), _F32),
                        pltpu.VMEM((rs, D_MODEL), _F32),
                        pltpu.VMEM((dec_batch, H_A, DK, DV), _F32),
                        pltpu.VMEM((rs, D_B), _F32),
                        pltpu.SemaphoreType.DMA((5,))],
        compiler_params=pltpu.CompilerParams(
            dimension_semantics=("arbitrary",), vmem_limit_bytes=VMEM_LIMIT_BYTES),
        name="layer_step",
    )(x_prompt, x_sample.reshape(rs, D_MODEL), state_hgrn[0],
      row(norm_w[0]), w_in.reshape(D_MODEL, IN_WIDTH), lb_logits, row(g_norm_w[0]), row(ln_v_w[0]), row(ln_v_b[0]),
      w_s[0], b_s[0], w_out.reshape(D_MODEL, D_MODEL), row(final_norm_w))

    return (y_prompt,
            y_sample.reshape(dec_batch, dec_seq, D_MODEL),
            state_p[None],
            state_s[None],
            v_sample.reshape(1, dec_batch, dec_seq, D_B))
```

```python
import functools
import math

import jax
import jax.numpy as jnp
from jax import lax
from jax.experimental import pallas as pl
from jax.experimental.pallas import tpu as pltpu

D_MODEL = 1024
D_A = 512
D_B = 512
H_A = 4
DK = 128
DV = 128
H_B = 4
C_B = 128
MLP_CHUNK = 128
HGRN_CHUNK = 64
IN_WIDTH = 4 * D_A + 3 * D_B
EPS = 1e-6

PROMPT_BLOCK = 512
BLOCKS_PER_STEP = 2
PROMPT_GROUP = 128
PROMPT_CHUNK = 128
SUBLANES = 8
LOG_DECAY_LIMIT = 78.0
VMEM_LIMIT_BYTES = 60 * 1024 * 1024
PROJECTION_PLAN = {0: (1, 0), 1: (2, 4), 3: (5,), 5: (3, 6)}

WEIGHT_STAGE_COLS = 256
WEIGHT_STAGE_SLABS = 3
OUT_PARTS = 4
PROMPT_SLOTS = 3

_NT = (((1,), (1,)), ((), ()))
_TN = (((0,), (0,)), ((), ()))
_F32 = jnp.float32
_BF16 = jnp.bfloat16


def _dot(a, b, dims=None):
    if dims is None:
        return jnp.dot(a, b, preferred_element_type=_F32)
    return lax.dot_general(a, b, dims, preferred_element_type=_F32)


def _rms(x, w):
    return x * lax.rsqrt(jnp.mean(x * x, axis=-1, keepdims=True) + EPS) * w


def _half_tanh_half(x):
    h = 0.5 * x
    return jnp.tanh(h), h


def _silu(x):
    t, h = _half_tanh_half(x)
    return h * t + h


def _gelu_exact(x):
    return 0.5 * x * (1.0 + lax.erf(x * math.sqrt(0.5)))


def _same_chunk_causal(rows, chunk):
    shift = chunk.bit_length() - 1
    assert chunk == 1 << shift
    t = lax.broadcasted_iota(jnp.int32, (rows, rows), 0)
    s = lax.broadcasted_iota(jnp.int32, (rows, rows), 1)
    same = lax.shift_right_logical(t, shift) == lax.shift_right_logical(s, shift)
    return jnp.logical_and(same, s <= t)


def _lane_cat(a, group):
    n = a.shape[0] // group
    if n == 1:
        return a
    return jnp.concatenate([a[gi * group:(gi + 1) * group] for gi in range(n)], axis=1)


def _lane_split(a, n):
    if n == 1:
        return a
    w = a.shape[1] // n
    return jnp.concatenate([a[:, gi * w:(gi + 1) * w] for gi in range(n)], axis=0)


def _chunk_cumsum(g, chunk):
    rows, width = g.shape
    n_regs = rows // SUBLANES
    x = g.reshape(n_regs, SUBLANES, width)
    sub = lax.broadcasted_iota(jnp.int32, x.shape, 1)
    shift = 1
    while shift < SUBLANES:
        x = x + jnp.where(sub >= shift, pltpu.roll(x, shift, 1), 0.0)
        shift *= 2
    regs_per_chunk = chunk // SUBLANES
    if regs_per_chunk > 1:
        parts, run = [], None
        for r in range(n_regs):
            part = x[r] + run if r % regs_per_chunk else x[r]
            run = part[SUBLANES - 1:SUBLANES, :]
            parts.append(part)
        return jnp.concatenate(parts, axis=0)
    return x.reshape(rows, width)


def _recurrence_fast(qs, kk, g, iv, side, *, states_t, group, chunk, carry_state):
    rows = qs.shape[0]
    n_groups = rows // group
    chunks_per_group = group // chunk
    n_chunks = rows // chunk
    blk_mask = _same_chunk_causal(group, chunk)
    b = _chunk_cumsum(g, chunk)
    side(1)

    iv = iv().astype(_BF16)
    q_dec = (qs * jnp.exp(b)).astype(_BF16)
    k_inv = kk * jnp.exp(-b)
    k_inv_bf = k_inv.astype(_BF16)

    low = None
    for c in range(n_chunks):
        b_last = b[(c + 1) * chunk - 1:(c + 1) * chunk, :]
        low = b_last if low is None else jnp.minimum(low, b_last)

    if not carry_state:
        ends = [b[(c + 1) * chunk - 1:(c + 1) * chunk, :] for c in range(n_chunks)]
        ends.append(jnp.zeros((DK - n_chunks, b.shape[1]), _F32))
        dec_cols = jnp.exp(jnp.concatenate(ends, axis=0)).T

    new_states = [[None] * H_A for _ in range(n_chunks)]
    o_heads = []
    for hd in range(H_A):
        ln = slice(hd * DK, (hd + 1) * DK)
        groups = [slice(gi * group, (gi + 1) * group) for gi in range(n_groups)]
        chunks = [slice(c * chunk, (c + 1) * chunk) for c in range(n_chunks)]
        scores = [jnp.where(blk_mask, _dot(q_dec[rg, ln], k_inv_bf[rg, ln], _NT), 0.0).astype(_BF16)
                  for rg in groups]
        dec_last, grow = [], []
        for c, rw in enumerate(chunks):
            last = (c + 1) * chunk - 1
            dec_last.append(jnp.exp(b[last:last + 1, ln]))
            k_upd = (k_inv[rw, ln] * dec_last[c]).astype(_BF16)
            if carry_state:
                grow.append(_dot(iv[rw, ln], k_upd, _TN))
            else:
                grow.append(_dot(k_upd, iv[rw, ln], _TN))
        intra = [_dot(sc, iv[rg, ln]) for sc, rg in zip(scores, groups)]
        o_chunks = []
        s_t = states_t[0][hd]
        for c, rw in enumerate(chunks):
            if carry_state:
                inter = _dot(q_dec[rw, ln], s_t.astype(_BF16), _NT)
                s_t = s_t * dec_last[c] + grow[c]
                new_states[c][hd] = s_t
            else:
                s = states_t[c][hd]
                inter = _dot(q_dec[rw, ln], s.astype(_BF16))
                new_states[c][hd] = s * dec_cols[ln, c:c + 1] + grow[c]
            gi, ci = divmod(c, chunks_per_group)
            o_chunks.append(inter + intra[gi][ci * chunk:(ci + 1) * chunk])
        o_heads.append(jnp.concatenate(o_chunks, axis=0) if n_chunks > 1 else o_chunks[0])
        side(2 + hd)
    return jnp.concatenate(o_heads, axis=1), (new_states, jnp.min(low))


def _recurrence_exact(qs, kk, g, iv, side, *, work_ref, state_ref, group, chunk):
    rows = qs.shape[0]
    assert chunk % SUBLANES == 0 and group % SUBLANES == 0
    minis_per_chunk = chunk // SUBLANES
    b = _chunk_cumsum(g, SUBLANES)
    half = work_ref.shape[1] // 2
    assert rows <= half and work_ref.shape[0] * 2 >= 5

    def staged(k, r0, n):
        return work_ref.at[k // 2, pl.ds((k % 2) * half + r0, n), :]

    for k, a in enumerate((qs, kk, b, iv())):
        staged(k, 0, rows)[...] = a
    t_idx = lax.broadcasted_iota(jnp.int32, (SUBLANES, DK), 0)

    def mini(m, carry):
        r0 = pl.multiple_of(m * SUBLANES, SUBLANES)
        qs8, kk8, b8, iv8 = (staged(k, r0, SUBLANES)[...] for k in range(4))
        c = m // minis_per_chunk
        for hd in range(H_A):
            ln = slice(hd * DK, (hd + 1) * DK)
            q_h, k_h, b_h, v_h = qs8[:, ln], kk8[:, ln], b8[:, ln], iv8[:, ln]
            s_t = state_ref[c, hd]
            acc = _dot(q_h * jnp.exp(b_h), s_t, _NT)
            for s in range(SUBLANES):
                dec = jnp.exp(jnp.minimum(b_h - b_h[s:s + 1, :], 0.0))
                term = jnp.where(t_idx >= s, q_h * k_h[s:s + 1, :] * dec, 0.0)
                acc = acc + jnp.sum(term, axis=1, keepdims=True) * v_h[s:s + 1, :]
            staged(4, r0, SUBLANES)[:, ln] = acc
            b_end = b_h[SUBLANES - 1:SUBLANES, :]
            k_upd = k_h * jnp.exp(b_end - b_h)
            state_ref[c, hd] = s_t * jnp.exp(b_end) + _dot(v_h, k_upd, _TN)
        return carry

    lax.fori_loop(0, rows // SUBLANES, mini, 0)
    return staged(4, 0, rows)[...], None


def _layer_block(x, refs, recurrence, *, group, mix_chunk):
    (norm_w_ref, w_in_ref, lb_logits_ref, g_norm_w_ref, ln_v_w_ref, ln_v_b_ref,
     w_s_ref, b_s_ref, w_out_ref, final_w_ref) = refs
    n_groups = x.shape[0] // group

    h = _rms(x, norm_w_ref[...]).astype(_BF16)
    slices = {}

    def proj(k):
        if k not in slices:
            slices[k] = _dot(h, w_in_ref[:, k * D_A:(k + 1) * D_A])
        return slices[k]

    def side(point):
        for k in PROJECTION_PLAN.get(point, ()):
            proj(k)

    logits = lb_logits_ref[...]
    e = jnp.exp(logits - jnp.max(logits, axis=0, keepdims=True))
    lb = e[0:1, :] / jnp.sum(e, axis=0, keepdims=True)

    side(0)
    half_span = 0.5 * (1.0 - lb)
    f = (lb + half_span) + half_span * _half_tanh_half(proj(1))[0]
    qs = _silu(proj(0)) * (DK ** -0.5)
    o, aux = recurrence(qs, 1.0 - f, jnp.log(f), lambda: proj(2), side)

    ga = proj(3)
    y_a_heads = []
    for hd in range(H_A):
        o_h = o[:, hd * DV:(hd + 1) * DV]
        y_a_heads.append(o_h * lax.rsqrt(jnp.mean(o_h * o_h, axis=-1, keepdims=True) + EPS))
    y_a = jnp.concatenate(y_a_heads, axis=1) * g_norm_w_ref[...] * _silu(ga)

    u = _gelu_exact(proj(4))
    v = _gelu_exact(proj(5))
    mu = jnp.mean(v, axis=-1, keepdims=True)
    vc = v - mu
    var = jnp.mean(vc * vc, axis=-1, keepdims=True)
    v_norm = vc * lax.rsqrt(var + EPS) * ln_v_w_ref[...] + ln_v_b_ref[...]
    gb = proj(6)
    v_bf = v_norm.astype(_BF16)
    mix_mask = _same_chunk_causal(group, mix_chunk)
    pos_r = lax.broadcasted_iota(jnp.int32, (group, MLP_CHUNK), 0)
    pos_c = lax.broadcasted_iota(jnp.int32, (group, MLP_CHUNK), 1)
    pos_mask = jnp.bitwise_and(pos_r, mix_chunk - 1) == pos_c
    mix_heads = []
    for hb in range(H_B):
        ln = slice(hb * C_B, (hb + 1) * C_B)
        if group == mix_chunk == MLP_CHUNK:
            w_full = w_s_ref[hb]
        else:
            w_rows = jnp.concatenate([w_s_ref[hb][:mix_chunk]] * (group // mix_chunk), axis=0)
            w_full = _dot(w_rows.astype(_BF16), jnp.where(pos_mask, 1.0, 0.0).astype(_BF16), _NT)
        w = jnp.where(mix_mask, w_full, 0.0).astype(_BF16)
        bias = jnp.sum(jnp.where(pos_mask, b_s_ref[hb:hb + 1, :], 0.0), axis=1, keepdims=True)
        mixed = _dot(w, _lane_cat(v_bf[:, ln], group)) + bias
        mix_heads.append(_lane_split(mixed, n_groups))
    mix = jnp.concatenate(mix_heads, axis=1)
    y_b = u * mix * _silu(gb)

    y = jnp.concatenate([y_a, y_b], axis=1).astype(_BF16)
    part = max(x.shape[0] // OUT_PARTS, group)
    outs = []
    for r0 in range(0, x.shape[0], part):
        out = x[r0:r0 + part] + _dot(y[r0:r0 + part], w_out_ref[...])
        outs.append(_rms(out, final_w_ref[...]))
    return jnp.concatenate(outs, axis=0), v_norm, aux


def _stage_weights_bf16(pairs, stage_ref, sem_ref):
    chunks = [(src, dst, c0) for src, dst in pairs for c0 in range(0, src.shape[1], WEIGHT_STAGE_COLS)]
    per_slab = stage_ref.shape[2] // WEIGHT_STAGE_COLS
    slots = stage_ref.shape[0] * per_slab
    ahead = slots - 1

    def slot(i):
        s = i % slots
        return stage_ref.at[s // per_slab, :, pl.ds((s % per_slab) * WEIGHT_STAGE_COLS, WEIGHT_STAGE_COLS)]

    def copy(i):
        src, _, c0 = chunks[i]
        return pltpu.make_async_copy(src.at[:, c0:c0 + WEIGHT_STAGE_COLS], slot(i), sem_ref.at[i % slots])

    for i in range(min(ahead, len(chunks))):
        copy(i).start()
    for i, (_, dst, c0) in enumerate(chunks):
        if i + ahead < len(chunks):
            copy(i + ahead).start()
        copy(i).wait()
        dst[:, c0:c0 + WEIGHT_STAGE_COLS] = slot(i)[...].astype(_BF16)


def _fused_kernel(xp_hbm, xs_hbm, state_in_hbm, norm_w_ref, w_in_hbm, lb_logits_ref, g_norm_w_ref,
                  ln_v_w_ref, ln_v_b_ref, w_s_ref, b_s_ref, w_out_hbm, final_w_ref,
                  yp_hbm, state_p_ref, ys_hbm, state_s_hbm, v_hbm,
                  state_t_scr, state_prev_scr, w_in_ref, w_out_ref, stage_scr, stage_sem,
                  xs_ref, state_in_ref, ys_ref, state_s_ref, v_ref, io_sem,
                  x_buf, x_sem, y_buf, y_sem,
                  *, sample_step, last_step, steps_per_stream, n_streams, frames):
    step = pl.program_id(0)
    blocks_per_stream = steps_per_stream * BLOCKS_PER_STEP
    last_block = xp_hbm.shape[0] * blocks_per_stream - 1
    prompt_step = jnp.where(step > sample_step, step - 1, step)

    def block_rows(hbm, g):
        r0 = pl.multiple_of(lax.rem(g, blocks_per_stream) * PROMPT_BLOCK, PROMPT_BLOCK)
        return hbm.at[g // blocks_per_stream, pl.ds(r0, PROMPT_BLOCK), :]

    def x_copy(g, into):
        slot = lax.rem(into, PROMPT_SLOTS)
        return pltpu.make_async_copy(block_rows(xp_hbm, g), x_buf.at[slot], x_sem.at[slot])

    def y_copy(g):
        slot = lax.rem(g, PROMPT_SLOTS)
        return pltpu.make_async_copy(y_buf.at[slot], block_rows(yp_hbm, g), y_sem.at[slot])
    refs = (norm_w_ref, w_in_ref, lb_logits_ref, g_norm_w_ref, ln_v_w_ref, ln_v_b_ref,
            w_s_ref, b_s_ref, w_out_ref, final_w_ref)
    sample_in = [pltpu.make_async_copy(src, dst, io_sem.at[i])
                 for i, (src, dst) in enumerate([(xs_hbm, xs_ref), (state_in_hbm, state_in_ref)])]
    sample_out = [pltpu.make_async_copy(src, dst, io_sem.at[len(sample_in) + i])
                  for i, (src, dst) in enumerate([(ys_ref, ys_hbm), (state_s_ref, state_s_hbm), (v_ref, v_hbm)])]

    @pl.when(step == 0)
    def _():
        x_copy(0, 0).start()
        _stage_weights_bf16([(w_in_hbm, w_in_ref), (w_out_hbm, w_out_ref)], stage_scr, stage_sem)
        for copy in sample_in:
            copy.start()

    def too_steep(low):
        return jnp.logical_not(low >= -LOG_DECAY_LIMIT)

    @pl.when(step != sample_step)
    def _prompt():
        j = lax.rem(prompt_step, steps_per_stream)
        first = prompt_step * BLOCKS_PER_STEP

        @pl.when(prompt_step >= 1)
        def _():
            y_copy(first - 2).wait()

        @pl.when(prompt_step >= 2)
        def _():
            y_copy(first - 3).wait()

        @pl.when(j == 0)
        def _():
            state_t_scr[...] = jnp.zeros_like(state_t_scr)

        state_prev_scr[...] = state_t_scr[...]
        states = [state_t_scr[0, hd] for hd in range(H_A)]
        low = None
        for blk in range(BLOCKS_PER_STEP):
            g = first + blk
            slot = lax.rem(g, PROMPT_SLOTS)
            x_copy(jnp.minimum(g + 1, last_block), g + 1).start()
            x_copy(g, g).wait()
            fast = functools.partial(_recurrence_fast, states_t=[states], group=PROMPT_GROUP,
                                     chunk=PROMPT_CHUNK, carry_state=True)
            y, _, (new_states, low_blk) = _layer_block(x_buf[slot], refs, fast,
                                                       group=PROMPT_GROUP, mix_chunk=MLP_CHUNK)
            y_buf[slot] = y
            if blk < BLOCKS_PER_STEP - 1:
                y_copy(g).start()
            states = new_states[-1]
            low = low_blk if low is None else jnp.minimum(low, low_blk)
        for hd in range(H_A):
            state_t_scr[0, hd] = states[hd]

        @pl.when(too_steep(low))
        def _():
            state_t_scr[...] = state_prev_scr[...]
            exact = functools.partial(_recurrence_exact, work_ref=stage_scr, state_ref=state_t_scr,
                                      group=PROMPT_GROUP, chunk=PROMPT_BLOCK)
            for blk in range(BLOCKS_PER_STEP):
                g = first + blk
                slot = lax.rem(g, PROMPT_SLOTS)
                if blk < BLOCKS_PER_STEP - 1:
                    y_copy(g).wait()
                y_exact, _, _ = _layer_block(x_buf[slot], refs, exact,
                                             group=PROMPT_GROUP, mix_chunk=MLP_CHUNK)
                y_buf[slot] = y_exact
                if blk < BLOCKS_PER_STEP - 1:
                    y_copy(g).start()

        y_copy(first + BLOCKS_PER_STEP - 1).start()

        @pl.when(j == steps_per_stream - 1)
        def _():
            for hd in range(H_A):
                state_p_ref[0, hd] = state_t_scr[0, hd].T

    @pl.when(step == sample_step)
    def _sample():
        for copy in sample_in:
            copy.wait()
        rows = n_streams * frames
        fast = functools.partial(
            _recurrence_fast,
            states_t=[[state_in_ref[s, hd] for hd in range(H_A)] for s in range(n_streams)],
            group=rows, chunk=frames, carry_state=False)
        y, v_norm, (new_states, low) = _layer_block(xs_ref[...], refs, fast, group=rows, mix_chunk=frames)
        ys_ref[...] = y
        v_ref[...] = v_norm
        for s in range(n_streams):
            for hd in range(H_A):
                state_s_ref[s, hd] = new_states[s][hd]

        @pl.when(too_steep(low))
        def _():
            for s in range(n_streams):
                for hd in range(H_A):
                    state_s_ref[s, hd] = state_in_ref[s, hd].T
            exact = functools.partial(_recurrence_exact, work_ref=stage_scr, state_ref=state_s_ref,
                                      group=rows, chunk=frames)
            y_exact, v_exact, _ = _layer_block(xs_ref[...], refs, exact, group=rows, mix_chunk=frames)
            ys_ref[...] = y_exact
            v_ref[...] = v_exact
            for s in range(n_streams):
                for hd in range(H_A):
                    state_s_ref[s, hd] = state_s_ref[s, hd].T

        for copy in sample_out:
            copy.start()

    @pl.when(step == last_step)
    def _():
        for copy in sample_out:
            copy.wait()
        x_copy(last_block, last_block + 1).wait()
        for g in range(last_block - BLOCKS_PER_STEP, last_block + 1):
            y_copy(g).wait()


def _full(shape):
    return pl.BlockSpec(shape, lambda *_: (0,) * len(shape))


def kernel(x_prompt, x_sample, state_hgrn, norm_w, w_in, lb_logits, g_norm_w, ln_v_w, ln_v_b, w_s, b_s, w_out, final_norm_w):
    batch, seq, _ = x_prompt.shape
    dec_batch, dec_seq, _ = x_sample.shape
    assert norm_w.shape[0] == 1, "single-layer trunk"
    assert seq % (PROMPT_BLOCK * BLOCKS_PER_STEP) == 0 and PROMPT_BLOCK % PROMPT_GROUP == 0
    assert PROMPT_GROUP % MLP_CHUNK == 0 and MLP_CHUNK % PROMPT_CHUNK == 0
    assert dec_seq <= HGRN_CHUNK and dec_seq <= MLP_CHUNK and dec_seq & (dec_seq - 1) == 0
    assert w_s.shape[2:] == (MLP_CHUNK, MLP_CHUNK)

    rp = PROMPT_BLOCK * BLOCKS_PER_STEP
    rs = dec_batch * dec_seq
    steps_per_stream = seq // rp
    prompt_steps = batch * steps_per_stream
    assert prompt_steps >= 2
    row = lambda a: a.reshape(1, -1)

    sample_step = prompt_steps // 2

    def y_map(step):
        blk = jnp.where(step >= sample_step, step - 1, step)
        return blk // steps_per_stream, blk % steps_per_stream, 0

    def state_map(step):
        return y_map(step)[0], 0, 0, 0

    any_space = pl.BlockSpec(memory_space=pl.ANY)

    y_prompt, state_p, y_sample, state_s, v_sample = pl.pallas_call(
        functools.partial(_fused_kernel, sample_step=sample_step, last_step=prompt_steps,
                          steps_per_stream=steps_per_stream, n_streams=dec_batch, frames=dec_seq),
        grid=(prompt_steps + 1,),
        in_specs=[any_space, any_space, any_space,
                  _full((1, D_MODEL)), any_space, _full(lb_logits.shape),
                  _full((1, D_A)), _full((1, D_B)), _full((1, D_B)),
                  _full((H_B, MLP_CHUNK, MLP_CHUNK)), _full((H_B, MLP_CHUNK)),
                  any_space, _full((1, D_MODEL))],
        out_specs=[any_space,
                   pl.BlockSpec((1, H_A, DK, DV), state_map),
                   any_space, any_space, any_space],
        out_shape=[jax.ShapeDtypeStruct((batch, seq, D_MODEL), _F32),
                   jax.ShapeDtypeStruct((batch, H_A, DK, DV), _F32),
                   jax.ShapeDtypeStruct((rs, D_MODEL), _F32),
                   jax.ShapeDtypeStruct((dec_batch, H_A, DK, DV), _F32),
                   jax.ShapeDtypeStruct((rs, D_B), _F32)],
        scratch_shapes=[pltpu.VMEM((1, H_A, DV, DK), _F32),
                        pltpu.VMEM((1, H_A, DV, DK), _F32),
                        pltpu.VMEM((D_MODEL, IN_WIDTH), _BF16),
                        pltpu.VMEM((D_MODEL, D_MODEL), _BF16),
                        pltpu.VMEM((WEIGHT_STAGE_SLABS, D_MODEL, D_A), _F32),
                        pltpu.SemaphoreType.DMA((WEIGHT_STAGE_SLABS * (D_A // WEIGHT_STAGE_COLS),)),
                        pltpu.VMEM((rs, D_MODEL), _F32),
                        pltpu.VMEM((dec_batch, H_A, DK, DV), _F32),
                        pltpu.VMEM((rs, D_MODEL), _F32),
                        pltpu.VMEM((dec_batch, H_A, DK, DV), _F32),
                        pltpu.VMEM((rs, D_B), _F32),
                        pltpu.SemaphoreType.DMA((5,)),
                        pltpu.VMEM((PROMPT_SLOTS, PROMPT_BLOCK, D_MODEL), _F32),
                        pltpu.SemaphoreType.DMA((PROMPT_SLOTS,)),
                        pltpu.VMEM((PROMPT_SLOTS, PROMPT_BLOCK, D_MODEL), _F32),
                        pltpu.SemaphoreType.DMA((PROMPT_SLOTS,))],
        compiler_params=pltpu.CompilerParams(
            dimension_semantics=("arbitrary",), vmem_limit_bytes=VMEM_LIMIT_BYTES),
        name="layer_step",
    )(x_prompt, x_sample.reshape(rs, D_MODEL), state_hgrn[0],
      row(norm_w[0]), w_in.reshape(D_MODEL, IN_WIDTH), lb_logits, row(g_norm_w[0]), row(ln_v_w[0]), row(ln_v_b[0]),
      w_s[0], b_s[0], w_out.reshape(D_MODEL, D_MODEL), row(final_norm_w))

    return (y_prompt,
            y_sample.reshape(dec_batch, dec_seq, D_MODEL),
            state_p[None],
            state_s[None],
            v_sample.reshape(1, dec_batch, dec_seq, D_B))
```

```python
import functools
import math

import jax
import jax.numpy as jnp
from jax import lax
from jax.experimental import pallas as pl
from jax.experimental.pallas import tpu as pltpu

D_MODEL = 1024
D_A = 512
D_B = 512
H_A = 4
DK = 128
DV = 128
H_B = 4
C_B = 128
MLP_CHUNK = 128
HGRN_CHUNK = 64
IN_WIDTH = 4 * D_A + 3 * D_B
EPS = 1e-6

PROMPT_BLOCK = 512
BLOCKS_PER_STEP = 2
PROMPT_GROUP = 128
PROMPT_CHUNK = 128
SUBLANES = 8
LOG_DECAY_LIMIT = 78.0
VMEM_LIMIT_BYTES = 60 * 1024 * 1024
PROJECTION_PLAN = {0: (1, 0), 1: (2, 4), 3: (5,), 5: (3, 6)}

WEIGHT_STAGE_COLS = 256
WEIGHT_STAGE_SLABS = 3
OUT_PARTS = 4
PROMPT_SLOTS = 2 * BLOCKS_PER_STEP

_NT = (((1,), (1,)), ((), ()))
_TN = (((0,), (0,)), ((), ()))
_F32 = jnp.float32
_BF16 = jnp.bfloat16


def _dot(a, b, dims=None):
    if dims is None:
        return jnp.dot(a, b, preferred_element_type=_F32)
    return lax.dot_general(a, b, dims, preferred_element_type=_F32)


def _rms(x, w):
    return x * lax.rsqrt(jnp.mean(x * x, axis=-1, keepdims=True) + EPS) * w


def _half_tanh_half(x):
    h = 0.5 * x
    return jnp.tanh(h), h


def _silu(x):
    t, h = _half_tanh_half(x)
    return h * t + h


def _gelu_exact(x):
    return 0.5 * x * (1.0 + lax.erf(x * math.sqrt(0.5)))


def _same_chunk_causal(rows, chunk):
    shift = chunk.bit_length() - 1
    assert chunk == 1 << shift
    t = lax.broadcasted_iota(jnp.int32, (rows, rows), 0)
    s = lax.broadcasted_iota(jnp.int32, (rows, rows), 1)
    same = lax.shift_right_logical(t, shift) == lax.shift_right_logical(s, shift)
    return jnp.logical_and(same, s <= t)


def _lane_cat(a, group):
    n = a.shape[0] // group
    if n == 1:
        return a
    return jnp.concatenate([a[gi * group:(gi + 1) * group] for gi in range(n)], axis=1)


def _lane_split(a, n):
    if n == 1:
        return a
    w = a.shape[1] // n
    return jnp.concatenate([a[:, gi * w:(gi + 1) * w] for gi in range(n)], axis=0)


def _chunk_cumsum(g, chunk):
    rows, width = g.shape
    n_regs = rows // SUBLANES
    x = g.reshape(n_regs, SUBLANES, width)
    sub = lax.broadcasted_iota(jnp.int32, x.shape, 1)
    shift = 1
    while shift < SUBLANES:
        x = x + jnp.where(sub >= shift, pltpu.roll(x, shift, 1), 0.0)
        shift *= 2
    regs_per_chunk = chunk // SUBLANES
    if regs_per_chunk > 1:
        parts, run = [], None
        for r in range(n_regs):
            part = x[r] + run if r % regs_per_chunk else x[r]
            run = part[SUBLANES - 1:SUBLANES, :]
            parts.append(part)
        return jnp.concatenate(parts, axis=0)
    return x.reshape(rows, width)


def _recurrence_fast(qs, kk, g, iv, side, *, states_t, group, chunk, carry_state):
    rows = qs.shape[0]
    n_groups = rows // group
    chunks_per_group = group // chunk
    n_chunks = rows // chunk
    blk_mask = _same_chunk_causal(group, chunk)
    b = _chunk_cumsum(g, chunk)
    side(1)

    iv = iv().astype(_BF16)
    q_dec = (qs * jnp.exp(b)).astype(_BF16)
    k_inv = kk * jnp.exp(-b)
    k_inv_bf = k_inv.astype(_BF16)

    low = None
    for c in range(n_chunks):
        b_last = b[(c + 1) * chunk - 1:(c + 1) * chunk, :]
        low = b_last if low is None else jnp.minimum(low, b_last)

    if not carry_state:
        ends = [b[(c + 1) * chunk - 1:(c + 1) * chunk, :] for c in range(n_chunks)]
        ends.append(jnp.zeros((DK - n_chunks, b.shape[1]), _F32))
        dec_cols = jnp.exp(jnp.concatenate(ends, axis=0)).T

    new_states = [[None] * H_A for _ in range(n_chunks)]
    o_heads = []
    for hd in range(H_A):
        ln = slice(hd * DK, (hd + 1) * DK)
        groups = [slice(gi * group, (gi + 1) * group) for gi in range(n_groups)]
        chunks = [slice(c * chunk, (c + 1) * chunk) for c in range(n_chunks)]
        scores = [jnp.where(blk_mask, _dot(q_dec[rg, ln], k_inv_bf[rg, ln], _NT), 0.0).astype(_BF16)
                  for rg in groups]
        dec_last, grow = [], []
        for c, rw in enumerate(chunks):
            last = (c + 1) * chunk - 1
            dec_last.append(jnp.exp(b[last:last + 1, ln]))
            k_upd = (k_inv[rw, ln] * dec_last[c]).astype(_BF16)
            if carry_state:
                grow.append(_dot(iv[rw, ln], k_upd, _TN))
            else:
                grow.append(_dot(k_upd, iv[rw, ln], _TN))
        intra = [_dot(sc, iv[rg, ln]) for sc, rg in zip(scores, groups)]
        o_chunks = []
        s_t = states_t[0][hd]
        for c, rw in enumerate(chunks):
            if carry_state:
                inter = _dot(q_dec[rw, ln], s_t.astype(_BF16), _NT)
                s_t = s_t * dec_last[c] + grow[c]
                new_states[c][hd] = s_t
            else:
                s = states_t[c][hd]
                inter = _dot(q_dec[rw, ln], s.astype(_BF16))
                new_states[c][hd] = s * dec_cols[ln, c:c + 1] + grow[c]
            gi, ci = divmod(c, chunks_per_group)
            o_chunks.append(inter + intra[gi][ci * chunk:(ci + 1) * chunk])
        o_heads.append(jnp.concatenate(o_chunks, axis=0) if n_chunks > 1 else o_chunks[0])
        side(2 + hd)
    return jnp.concatenate(o_heads, axis=1), (new_states, jnp.min(low))


def _recurrence_exact(qs, kk, g, iv, side, *, work_ref, state_ref, group, chunk):
    rows = qs.shape[0]
    assert chunk % SUBLANES == 0 and group % SUBLANES == 0
    minis_per_chunk = chunk // SUBLANES
    b = _chunk_cumsum(g, SUBLANES)
    half = work_ref.shape[1] // 2
    assert rows <= half and work_ref.shape[0] * 2 >= 5

    def staged(k, r0, n):
        return work_ref.at[k // 2, pl.ds((k % 2) * half + r0, n), :]

    for k, a in enumerate((qs, kk, b, iv())):
        staged(k, 0, rows)[...] = a
    t_idx = lax.broadcasted_iota(jnp.int32, (SUBLANES, DK), 0)

    def mini(m, carry):
        r0 = pl.multiple_of(m * SUBLANES, SUBLANES)
        qs8, kk8, b8, iv8 = (staged(k, r0, SUBLANES)[...] for k in range(4))
        c = m // minis_per_chunk
        for hd in range(H_A):
            ln = slice(hd * DK, (hd + 1) * DK)
            q_h, k_h, b_h, v_h = qs8[:, ln], kk8[:, ln], b8[:, ln], iv8[:, ln]
            s_t = state_ref[c, hd]
            acc = _dot(q_h * jnp.exp(b_h), s_t, _NT)
            for s in range(SUBLANES):
                dec = jnp.exp(jnp.minimum(b_h - b_h[s:s + 1, :], 0.0))
                term = jnp.where(t_idx >= s, q_h * k_h[s:s + 1, :] * dec, 0.0)
                acc = acc + jnp.sum(term, axis=1, keepdims=True) * v_h[s:s + 1, :]
            staged(4, r0, SUBLANES)[:, ln] = acc
            b_end = b_h[SUBLANES - 1:SUBLANES, :]
            k_upd = k_h * jnp.exp(b_end - b_h)
            state_ref[c, hd] = s_t * jnp.exp(b_end) + _dot(v_h, k_upd, _TN)
        return carry

    lax.fori_loop(0, rows // SUBLANES, mini, 0)
    return staged(4, 0, rows)[...], None


def _layer_block(x, refs, recurrence, *, group, mix_chunk):
    (norm_w_ref, w_in_ref, lb_logits_ref, g_norm_w_ref, ln_v_w_ref, ln_v_b_ref,
     w_s_ref, b_s_ref, w_out_ref, final_w_ref) = refs
    n_groups = x.shape[0] // group

    h = _rms(x, norm_w_ref[...]).astype(_BF16)
    slices = {}

    def proj(k):
        if k not in slices:
            slices[k] = _dot(h, w_in_ref[:, k * D_A:(k + 1) * D_A])
        return slices[k]

    def side(point):
        for k in PROJECTION_PLAN.get(point, ()):
            proj(k)

    logits = lb_logits_ref[...]
    e = jnp.exp(logits - jnp.max(logits, axis=0, keepdims=True))
    lb = e[0:1, :] / jnp.sum(e, axis=0, keepdims=True)

    side(0)
    half_span = 0.5 * (1.0 - lb)
    f = (lb + half_span) + half_span * _half_tanh_half(proj(1))[0]
    qs = _silu(proj(0)) * (DK ** -0.5)
    o, aux = recurrence(qs, 1.0 - f, jnp.log(f), lambda: proj(2), side)

    ga = proj(3)
    y_a_heads = []
    for hd in range(H_A):
        o_h = o[:, hd * DV:(hd + 1) * DV]
        y_a_heads.append(o_h * lax.rsqrt(jnp.mean(o_h * o_h, axis=-1, keepdims=True) + EPS))
    y_a = jnp.concatenate(y_a_heads, axis=1) * g_norm_w_ref[...] * _silu(ga)

    u = _gelu_exact(proj(4))
    v = _gelu_exact(proj(5))
    mu = jnp.mean(v, axis=-1, keepdims=True)
    vc = v - mu
    var = jnp.mean(vc * vc, axis=-1, keepdims=True)
    v_norm = vc * lax.rsqrt(var + EPS) * ln_v_w_ref[...] + ln_v_b_ref[...]
    gb = proj(6)
    v_bf = v_norm.astype(_BF16)
    mix_mask = _same_chunk_causal(group, mix_chunk)
    pos_r = lax.broadcasted_iota(jnp.int32, (group, MLP_CHUNK), 0)
    pos_c = lax.broadcasted_iota(jnp.int32, (group, MLP_CHUNK), 1)
    pos_mask = jnp.bitwise_and(pos_r, mix_chunk - 1) == pos_c
    mix_heads = []
    for hb in range(H_B):
        ln = slice(hb * C_B, (hb + 1) * C_B)
        if group == mix_chunk == MLP_CHUNK:
            w_full = w_s_ref[hb]
        else:
            w_rows = jnp.concatenate([w_s_ref[hb][:mix_chunk]] * (group // mix_chunk), axis=0)
            w_full = _dot(w_rows.astype(_BF16), jnp.where(pos_mask, 1.0, 0.0).astype(_BF16), _NT)
        w = jnp.where(mix_mask, w_full, 0.0).astype(_BF16)
        bias = jnp.sum(jnp.where(pos_mask, b_s_ref[hb:hb + 1, :], 0.0), axis=1, keepdims=True)
        mixed = _dot(w, _lane_cat(v_bf[:, ln], group)) + bias
        mix_heads.append(_lane_split(mixed, n_groups))
    mix = jnp.concatenate(mix_heads, axis=1)
    y_b = u * mix * _silu(gb)

    y = jnp.concatenate([y_a, y_b], axis=1).astype(_BF16)
    part = max(x.shape[0] // OUT_PARTS, group)
    outs = []
    for r0 in range(0, x.shape[0], part):
        out = x[r0:r0 + part] + _dot(y[r0:r0 + part], w_out_ref[...])
        outs.append(_rms(out, final_w_ref[...]))
    return jnp.concatenate(outs, axis=0), v_norm, aux


def _stage_weights_bf16(pairs, stage_ref, sem_ref):
    chunks = [(src, dst, c0) for src, dst in pairs for c0 in range(0, src.shape[1], WEIGHT_STAGE_COLS)]
    per_slab = stage_ref.shape[2] // WEIGHT_STAGE_COLS
    slots = stage_ref.shape[0] * per_slab
    ahead = slots - 1

    def slot(i):
        s = i % slots
        return stage_ref.at[s // per_slab, :, pl.ds((s % per_slab) * WEIGHT_STAGE_COLS, WEIGHT_STAGE_COLS)]

    def copy(i):
        src, _, c0 = chunks[i]
        return pltpu.make_async_copy(src.at[:, c0:c0 + WEIGHT_STAGE_COLS], slot(i), sem_ref.at[i % slots])

    for i in range(min(ahead, len(chunks))):
        copy(i).start()
    for i, (_, dst, c0) in enumerate(chunks):
        if i + ahead < len(chunks):
            copy(i + ahead).start()
        copy(i).wait()
        dst[:, c0:c0 + WEIGHT_STAGE_COLS] = slot(i)[...].astype(_BF16)


def _fused_kernel(xp_hbm, xs_hbm, state_in_hbm, norm_w_ref, w_in_hbm, lb_logits_ref, g_norm_w_ref,
                  ln_v_w_ref, ln_v_b_ref, w_s_ref, b_s_ref, w_out_hbm, final_w_ref,
                  yp_hbm, state_p_ref, ys_hbm, state_s_hbm, v_hbm,
                  state_t_scr, state_prev_scr, w_in_ref, w_out_ref, stage_scr, stage_sem,
                  xs_ref, state_in_ref, ys_ref, state_s_ref, v_ref, io_sem,
                  x_buf, x_sem, y_buf, y_sem,
                  *, sample_step, last_step, steps_per_stream, n_streams, frames):
    step = pl.program_id(0)
    blocks_per_stream = steps_per_stream * BLOCKS_PER_STEP
    last_block = xp_hbm.shape[0] * blocks_per_stream - 1
    prompt_step = jnp.where(step > sample_step, step - 1, step)

    def block_rows(hbm, g):
        r0 = pl.multiple_of(lax.rem(g, blocks_per_stream) * PROMPT_BLOCK, PROMPT_BLOCK)
        return hbm.at[g // blocks_per_stream, pl.ds(r0, PROMPT_BLOCK), :]

    def x_copy(g, into):
        slot = lax.rem(into, PROMPT_SLOTS)
        return pltpu.make_async_copy(block_rows(xp_hbm, g), x_buf.at[slot], x_sem.at[slot])

    def y_copy(g):
        slot = lax.rem(g, PROMPT_SLOTS)
        return pltpu.make_async_copy(y_buf.at[slot], block_rows(yp_hbm, g), y_sem.at[slot])
    refs = (norm_w_ref, w_in_ref, lb_logits_ref, g_norm_w_ref, ln_v_w_ref, ln_v_b_ref,
            w_s_ref, b_s_ref, w_out_ref, final_w_ref)
    sample_in = [pltpu.make_async_copy(src, dst, io_sem.at[i])
                 for i, (src, dst) in enumerate([(xs_hbm, xs_ref), (state_in_hbm, state_in_ref)])]
    sample_out = [pltpu.make_async_copy(src, dst, io_sem.at[len(sample_in) + i])
                  for i, (src, dst) in enumerate([(ys_ref, ys_hbm), (state_s_ref, state_s_hbm), (v_ref, v_hbm)])]

    @pl.when(step == 0)
    def _():
        for g in range(BLOCKS_PER_STEP):
            x_copy(g, g).start()
        _stage_weights_bf16([(w_in_hbm, w_in_ref), (w_out_hbm, w_out_ref)], stage_scr, stage_sem)
        for copy in sample_in:
            copy.start()

    def too_steep(low):
        return jnp.logical_not(low >= -LOG_DECAY_LIMIT)

    @pl.when(step != sample_step)
    def _prompt():
        j = lax.rem(prompt_step, steps_per_stream)
        first = prompt_step * BLOCKS_PER_STEP

        @pl.when(prompt_step >= 2)
        def _():
            for blk in range(BLOCKS_PER_STEP):
                y_copy(first - PROMPT_SLOTS + blk).wait()

        for blk in range(BLOCKS_PER_STEP):
            x_copy(first + blk, first + blk).wait()
        for blk in range(BLOCKS_PER_STEP, 2 * BLOCKS_PER_STEP):
            x_copy(jnp.minimum(first + blk, last_block), first + blk).start()

        @pl.when(j == 0)
        def _():
            state_t_scr[...] = jnp.zeros_like(state_t_scr)

        state_prev_scr[...] = state_t_scr[...]
        states = [state_t_scr[0, hd] for hd in range(H_A)]
        low = None
        for blk in range(BLOCKS_PER_STEP):
            g = first + blk
            slot = lax.rem(g, PROMPT_SLOTS)
            fast = functools.partial(_recurrence_fast, states_t=[states], group=PROMPT_GROUP,
                                     chunk=PROMPT_CHUNK, carry_state=True)
            y, _, (new_states, low_blk) = _layer_block(x_buf[slot], refs, fast,
                                                       group=PROMPT_GROUP, mix_chunk=MLP_CHUNK)
            y_buf[slot] = y
            states = new_states[-1]
            low = low_blk if low is None else jnp.minimum(low, low_blk)
        for hd in range(H_A):
            state_t_scr[0, hd] = states[hd]

        @pl.when(too_steep(low))
        def _():
            state_t_scr[...] = state_prev_scr[...]
            exact = functools.partial(_recurrence_exact, work_ref=stage_scr, state_ref=state_t_scr,
                                      group=PROMPT_GROUP, chunk=PROMPT_BLOCK)
            for blk in range(BLOCKS_PER_STEP):
                g = first + blk
                slot = lax.rem(g, PROMPT_SLOTS)
                y_exact, _, _ = _layer_block(x_buf[slot], refs, exact,
                                             group=PROMPT_GROUP, mix_chunk=MLP_CHUNK)
                y_buf[slot] = y_exact

        for blk in range(BLOCKS_PER_STEP):
            y_copy(first + blk).start()

        @pl.when(j == steps_per_stream - 1)
        def _():
            for hd in range(H_A):
                state_p_ref[0, hd] = state_t_scr[0, hd].T

    @pl.when(step == sample_step)
    def _sample():
        for copy in sample_in:
            copy.wait()
        rows = n_streams * frames
        fast = functools.partial(
            _recurrence_fast,
            states_t=[[state_in_ref[s, hd] for hd in range(H_A)] for s in range(n_streams)],
            group=rows, chunk=frames, carry_state=False)
        y, v_norm, (new_states, low) = _layer_block(xs_ref[...], refs, fast, group=rows, mix_chunk=frames)
        ys_ref[...] = y
        v_ref[...] = v_norm
        for s in range(n_streams):
            for hd in range(H_A):
                state_s_ref[s, hd] = new_states[s][hd]

        @pl.when(too_steep(low))
        def _():
            for s in range(n_streams):
                for hd in range(H_A):
                    state_s_ref[s, hd] = state_in_ref[s, hd].T
            exact = functools.partial(_recurrence_exact, work_ref=stage_scr, state_ref=state_s_ref,
                                      group=rows, chunk=frames)
            y_exact, v_exact, _ = _layer_block(xs_ref[...], refs, exact, group=rows, mix_chunk=frames)
            ys_ref[...] = y_exact
            v_ref[...] = v_exact
            for s in range(n_streams):
                for hd in range(H_A):
                    state_s_ref[s, hd] = state_s_ref[s, hd].T

        for copy in sample_out:
            copy.start()

    @pl.when(step == last_step)
    def _():
        for copy in sample_out:
            copy.wait()
        for g in range(last_block + 1, last_block + 1 + BLOCKS_PER_STEP):
            x_copy(last_block, g).wait()
        for g in range(last_block + 1 - PROMPT_SLOTS, last_block + 1):
            y_copy(g).wait()


def _full(shape):
    return pl.BlockSpec(shape, lambda *_: (0,) * len(shape))


def kernel(x_prompt, x_sample, state_hgrn, norm_w, w_in, lb_logits, g_norm_w, ln_v_w, ln_v_b, w_s, b_s, w_out, final_norm_w):
    batch, seq, _ = x_prompt.shape
    dec_batch, dec_seq, _ = x_sample.shape
    assert norm_w.shape[0] == 1, "single-layer trunk"
    assert seq % (PROMPT_BLOCK * BLOCKS_PER_STEP) == 0 and PROMPT_BLOCK % PROMPT_GROUP == 0
    assert PROMPT_GROUP % MLP_CHUNK == 0 and MLP_CHUNK % PROMPT_CHUNK == 0
    assert dec_seq <= HGRN_CHUNK and dec_seq <= MLP_CHUNK and dec_seq & (dec_seq - 1) == 0
    assert w_s.shape[2:] == (MLP_CHUNK, MLP_CHUNK)

    rp = PROMPT_BLOCK * BLOCKS_PER_STEP
    rs = dec_batch * dec_seq
    steps_per_stream = seq // rp
    prompt_steps = batch * steps_per_stream
    assert prompt_steps >= 2
    row = lambda a: a.reshape(1, -1)

    sample_step = prompt_steps // 2

    def y_map(step):
        blk = jnp.where(step >= sample_step, step - 1, step)
        return blk // steps_per_stream, blk % steps_per_stream, 0

    def state_map(step):
        return y_map(step)[0], 0, 0, 0

    any_space = pl.BlockSpec(memory_space=pl.ANY)

    y_prompt, state_p, y_sample, state_s, v_sample = pl.pallas_call(
        functools.partial(_fused_kernel, sample_step=sample_step, last_step=prompt_steps,
                          steps_per_stream=steps_per_stream, n_streams=dec_batch, frames=dec_seq),
        grid=(prompt_steps + 1,),
        in_specs=[any_space, any_space, any_space,
                  _full((1, D_MODEL)), any_space, _full(lb_logits.shape),
                  _full((1, D_A)), _full((1, D_B)), _full((1, D_B)),
                  _full((H_B, MLP_CHUNK, MLP_CHUNK)), _full((H_B, MLP_CHUNK)),
                  any_space, _full((1, D_MODEL))],
        out_specs=[any_space,
                   pl.BlockSpec((1, H_A, DK, DV), state_map),
                   any_space, any_space, any_space],
        out_shape=[jax.ShapeDtypeStruct((batch, seq, D_MODEL), _F32),
                   jax.ShapeDtypeStruct((batch, H_A, DK, DV), _F32),
                   jax.ShapeDtypeStruct((rs, D_MODEL), _F32),
                   jax.ShapeDtypeStruct((dec_batch, H_A, DK, DV), _F32),
                   jax.ShapeDtypeStruct((rs, D_B), _F32)],
        scratch_shapes=[pltpu.VMEM((1, H_A, DV, DK), _F32),
                        pltpu.VMEM((1, H_A, DV, DK), _F32),
                        pltpu.VMEM((D_MODEL, IN_WIDTH), _BF16),
                        pltpu.VMEM((D_MODEL, D_MODEL), _BF16),
                        pltpu.VMEM((WEIGHT_STAGE_SLABS, D_MODEL, D_A), _F32),
                        pltpu.SemaphoreType.DMA((WEIGHT_STAGE_SLABS * (D_A // WEIGHT_STAGE_COLS),)),
                        pltpu.VMEM((rs, D_MODEL), _F32),
                        pltpu.VMEM((dec_batch, H_A, DK, DV), _F32),
                        pltpu.VMEM((rs, D_MODEL), _F32),
                        pltpu.VMEM((dec_batch, H_A, DK, DV), _F32),
                        pltpu.VMEM((rs, D_B), _F32),
                        pltpu.SemaphoreType.DMA((5,)),
                        pltpu.VMEM((PROMPT_SLOTS, PROMPT_BLOCK, D_MODEL), _F32),
                        pltpu.SemaphoreType.DMA((PROMPT_SLOTS,)),
                        pltpu.VMEM((PROMPT_SLOTS, PROMPT_BLOCK, D_MODEL), _F32),
                        pltpu.SemaphoreType.DMA((PROMPT_SLOTS,))],
        compiler_params=pltpu.CompilerParams(
            dimension_semantics=("arbitrary",), vmem_limit_bytes=VMEM_LIMIT_BYTES),
        name="layer_step",
    )(x_prompt, x_sample.reshape(rs, D_MODEL), state_hgrn[0],
      row(norm_w[0]), w_in.reshape(D_MODEL, IN_WIDTH), lb_logits, row(g_norm_w[0]), row(ln_v_w[0]), row(ln_v_b[0]),
      w_s[0], b_s[0], w_out.reshape(D_MODEL, D_MODEL), row(final_norm_w))

    return (y_prompt,
            y_sample.reshape(dec_batch, dec_seq, D_MODEL),
            state_p[None],
            state_s[None],
            v_sample.reshape(1, dec_batch, dec_seq, D_B))
```

```python
import functools
import math

import jax
import jax.numpy as jnp
from jax import lax
from jax.experimental import pallas as pl
from jax.experimental.pallas import tpu as pltpu

D_MODEL = 1024
D_A = 512
D_B = 512
H_A = 4
DK = 128
DV = 128
H_B = 4
C_B = 128
MLP_CHUNK = 128
HGRN_CHUNK = 64
IN_WIDTH = 4 * D_A + 3 * D_B
EPS = 1e-6

PROMPT_BLOCK = 512
BLOCKS_PER_STEP = 2
PROMPT_GROUP = 128
PROMPT_CHUNK = 128
SUBLANES = 8
LOG_DECAY_LIMIT = 78.0
VMEM_LIMIT_BYTES = 60 * 1024 * 1024
PROJECTION_PLAN = {0: (1, 0), 1: (2, 4), 3: (5,), 5: (3, 6)}

WEIGHT_STAGE_COLS = 256
WEIGHT_STAGE_SLABS = 3
OUT_PARTS = 4

_NT = (((1,), (1,)), ((), ()))
_TN = (((0,), (0,)), ((), ()))
_F32 = jnp.float32
_BF16 = jnp.bfloat16


def _dot(a, b, dims=None):
    if dims is None:
        return jnp.dot(a, b, preferred_element_type=_F32)
    return lax.dot_general(a, b, dims, preferred_element_type=_F32)


def _rms(x, w):
    return x * lax.rsqrt(jnp.mean(x * x, axis=-1, keepdims=True) + EPS) * w


def _half_tanh_half(x):
    h = 0.5 * x
    return jnp.tanh(h), h


def _silu(x):
    t, h = _half_tanh_half(x)
    return h * t + h


def _gelu_exact(x):
    return 0.5 * x * (1.0 + lax.erf(x * math.sqrt(0.5)))


def _same_chunk_causal(rows, chunk):
    shift = chunk.bit_length() - 1
    assert chunk == 1 << shift
    t = lax.broadcasted_iota(jnp.int32, (rows, rows), 0)
    s = lax.broadcasted_iota(jnp.int32, (rows, rows), 1)
    same = lax.shift_right_logical(t, shift) == lax.shift_right_logical(s, shift)
    return jnp.logical_and(same, s <= t)


def _lane_cat(a, group):
    n = a.shape[0] // group
    if n == 1:
        return a
    return jnp.concatenate([a[gi * group:(gi + 1) * group] for gi in range(n)], axis=1)


def _lane_split(a, n):
    if n == 1:
        return a
    w = a.shape[1] // n
    return jnp.concatenate([a[:, gi * w:(gi + 1) * w] for gi in range(n)], axis=0)


def _chunk_cumsum(g, chunk):
    rows, width = g.shape
    n_regs = rows // SUBLANES
    x = g.reshape(n_regs, SUBLANES, width)
    sub = lax.broadcasted_iota(jnp.int32, x.shape, 1)
    shift = 1
    while shift < SUBLANES:
        x = x + jnp.where(sub >= shift, pltpu.roll(x, shift, 1), 0.0)
        shift *= 2
    regs_per_chunk = chunk // SUBLANES
    if regs_per_chunk > 1:
        parts, run = [], None
        for r in range(n_regs):
            part = x[r] + run if r % regs_per_chunk else x[r]
            run = part[SUBLANES - 1:SUBLANES, :]
            parts.append(part)
        return jnp.concatenate(parts, axis=0)
    return x.reshape(rows, width)


def _recurrence_fast(qs, kk, g, iv, side, *, states_t, group, chunk, carry_state):
    rows = qs.shape[0]
    n_groups = rows // group
    chunks_per_group = group // chunk
    n_chunks = rows // chunk
    blk_mask = _same_chunk_causal(group, chunk)
    b = _chunk_cumsum(g, chunk)
    side(1)

    iv = iv().astype(_BF16)
    q_dec = (qs * jnp.exp(b)).astype(_BF16)
    k_inv = kk * jnp.exp(-b)
    k_inv_bf = k_inv.astype(_BF16)

    low = None
    for c in range(n_chunks):
        b_last = b[(c + 1) * chunk - 1:(c + 1) * chunk, :]
        low = b_last if low is None else jnp.minimum(low, b_last)

    if not carry_state:
        ends = [b[(c + 1) * chunk - 1:(c + 1) * chunk, :] for c in range(n_chunks)]
        ends.append(jnp.zeros((DK - n_chunks, b.shape[1]), _F32))
        dec_cols = jnp.exp(jnp.concatenate(ends, axis=0)).T

    new_states = [[None] * H_A for _ in range(n_chunks)]
    o_heads = []
    for hd in range(H_A):
        ln = slice(hd * DK, (hd + 1) * DK)
        groups = [slice(gi * group, (gi + 1) * group) for gi in range(n_groups)]
        chunks = [slice(c * chunk, (c + 1) * chunk) for c in range(n_chunks)]
        scores = [jnp.where(blk_mask, _dot(q_dec[rg, ln], k_inv_bf[rg, ln], _NT), 0.0).astype(_BF16)
                  for rg in groups]
        dec_last, grow = [], []
        for c, rw in enumerate(chunks):
            last = (c + 1) * chunk - 1
            dec_last.append(jnp.exp(b[last:last + 1, ln]))
            k_upd = (k_inv[rw, ln] * dec_last[c]).astype(_BF16)
            if carry_state:
                grow.append(_dot(iv[rw, ln], k_upd, _TN))
            else:
                grow.append(_dot(k_upd, iv[rw, ln], _TN))
        intra = [_dot(sc, iv[rg, ln]) for sc, rg in zip(scores, groups)]
        o_chunks = []
        s_t = states_t[0][hd]
        for c, rw in enumerate(chunks):
            if carry_state:
                inter = _dot(q_dec[rw, ln], s_t.astype(_BF16), _NT)
                s_t = s_t * dec_last[c] + grow[c]
                new_states[c][hd] = s_t
            else:
                s = states_t[c][hd]
                inter = _dot(q_dec[rw, ln], s.astype(_BF16))
                new_states[c][hd] = s * dec_cols[ln, c:c + 1] + grow[c]
            gi, ci = divmod(c, chunks_per_group)
            o_chunks.append(inter + intra[gi][ci * chunk:(ci + 1) * chunk])
        o_heads.append(jnp.concatenate(o_chunks, axis=0) if n_chunks > 1 else o_chunks[0])
        side(2 + hd)
    return jnp.concatenate(o_heads, axis=1), (new_states, jnp.min(low))


def _recurrence_exact(qs, kk, g, iv, side, *, work_ref, state_ref, group, chunk):
    rows = qs.shape[0]
    assert chunk % SUBLANES == 0 and group % SUBLANES == 0
    minis_per_chunk = chunk // SUBLANES
    b = _chunk_cumsum(g, SUBLANES)
    half = work_ref.shape[1] // 2
    assert rows <= half and work_ref.shape[0] * 2 >= 5

    def staged(k, r0, n):
        return work_ref.at[k // 2, pl.ds((k % 2) * half + r0, n), :]

    for k, a in enumerate((qs, kk, b, iv())):
        staged(k, 0, rows)[...] = a
    t_idx = lax.broadcasted_iota(jnp.int32, (SUBLANES, DK), 0)

    def mini(m, carry):
        r0 = pl.multiple_of(m * SUBLANES, SUBLANES)
        qs8, kk8, b8, iv8 = (staged(k, r0, SUBLANES)[...] for k in range(4))
        c = m // minis_per_chunk
        for hd in range(H_A):
            ln = slice(hd * DK, (hd + 1) * DK)
            q_h, k_h, b_h, v_h = qs8[:, ln], kk8[:, ln], b8[:, ln], iv8[:, ln]
            s_t = state_ref[c, hd]
            acc = _dot(q_h * jnp.exp(b_h), s_t, _NT)
            for s in range(SUBLANES):
                dec = jnp.exp(jnp.minimum(b_h - b_h[s:s + 1, :], 0.0))
                term = jnp.where(t_idx >= s, q_h * k_h[s:s + 1, :] * dec, 0.0)
                acc = acc + jnp.sum(term, axis=1, keepdims=True) * v_h[s:s + 1, :]
            staged(4, r0, SUBLANES)[:, ln] = acc
            b_end = b_h[SUBLANES - 1:SUBLANES, :]
            k_upd = k_h * jnp.exp(b_end - b_h)
            state_ref[c, hd] = s_t * jnp.exp(b_end) + _dot(v_h, k_upd, _TN)
        return carry

    lax.fori_loop(0, rows // SUBLANES, mini, 0)
    return staged(4, 0, rows)[...], None


def _layer_block(x, refs, recurrence, *, group, mix_chunk):
    (norm_w_ref, w_in_ref, lb_logits_ref, g_norm_w_ref, ln_v_w_ref, ln_v_b_ref,
     w_s_ref, b_s_ref, w_out_ref, final_w_ref) = refs
    n_groups = x.shape[0] // group

    h = _rms(x, norm_w_ref[...]).astype(_BF16)
    slices = {}

    def proj(k):
        if k not in slices:
            slices[k] = _dot(h, w_in_ref[:, k * D_A:(k + 1) * D_A])
        return slices[k]

    def side(point):
        for k in PROJECTION_PLAN.get(point, ()):
            proj(k)

    logits = lb_logits_ref[...]
    e = jnp.exp(logits - jnp.max(logits, axis=0, keepdims=True))
    lb = e[0:1, :] / jnp.sum(e, axis=0, keepdims=True)

    side(0)
    half_span = 0.5 * (1.0 - lb)
    f = (lb + half_span) + half_span * _half_tanh_half(proj(1))[0]
    qs = _silu(proj(0)) * (DK ** -0.5)
    o, aux = recurrence(qs, 1.0 - f, jnp.log(f), lambda: proj(2), side)

    ga = proj(3)
    y_a_heads = []
    for hd in range(H_A):
        o_h = o[:, hd * DV:(hd + 1) * DV]
        y_a_heads.append(o_h * lax.rsqrt(jnp.mean(o_h * o_h, axis=-1, keepdims=True) + EPS))
    y_a = jnp.concatenate(y_a_heads, axis=1) * g_norm_w_ref[...] * _silu(ga)

    u = _gelu_exact(proj(4))
    v = _gelu_exact(proj(5))
    mu = jnp.mean(v, axis=-1, keepdims=True)
    vc = v - mu
    var = jnp.mean(vc * vc, axis=-1, keepdims=True)
    v_norm = vc * lax.rsqrt(var + EPS) * ln_v_w_ref[...] + ln_v_b_ref[...]
    gb = proj(6)
    v_bf = v_norm.astype(_BF16)
    mix_mask = _same_chunk_causal(group, mix_chunk)
    pos_r = lax.broadcasted_iota(jnp.int32, (group, MLP_CHUNK), 0)
    pos_c = lax.broadcasted_iota(jnp.int32, (group, MLP_CHUNK), 1)
    pos_mask = jnp.bitwise_and(pos_r, mix_chunk - 1) == pos_c
    mix_heads = []
    for hb in range(H_B):
        ln = slice(hb * C_B, (hb + 1) * C_B)
        if group == mix_chunk == MLP_CHUNK:
            w_full = w_s_ref[hb]
        else:
            w_rows = jnp.concatenate([w_s_ref[hb][:mix_chunk]] * (group // mix_chunk), axis=0)
            w_full = _dot(w_rows.astype(_BF16), jnp.where(pos_mask, 1.0, 0.0).astype(_BF16), _NT)
        w = jnp.where(mix_mask, w_full, 0.0).astype(_BF16)
        bias = jnp.sum(jnp.where(pos_mask, b_s_ref[hb:hb + 1, :], 0.0), axis=1, keepdims=True)
        mixed = _dot(w, _lane_cat(v_bf[:, ln], group)) + bias
        mix_heads.append(_lane_split(mixed, n_groups))
    mix = jnp.concatenate(mix_heads, axis=1)
    y_b = u * mix * _silu(gb)

    y = jnp.concatenate([y_a, y_b], axis=1).astype(_BF16)
    part = max(x.shape[0] // OUT_PARTS, group)
    outs = []
    for r0 in range(0, x.shape[0], part):
        out = x[r0:r0 + part] + _dot(y[r0:r0 + part], w_out_ref[...])
        outs.append(_rms(out, final_w_ref[...]))
    return jnp.concatenate(outs, axis=0), v_norm, aux


def _stage_weights_bf16(pairs, stage_ref, sem_ref):
    chunks = [(src, dst, c0) for src, dst in pairs for c0 in range(0, src.shape[1], WEIGHT_STAGE_COLS)]
    per_slab = stage_ref.shape[2] // WEIGHT_STAGE_COLS
    slots = stage_ref.shape[0] * per_slab
    ahead = slots - 1

    def slot(i):
        s = i % slots
        return stage_ref.at[s // per_slab, :, pl.ds((s % per_slab) * WEIGHT_STAGE_COLS, WEIGHT_STAGE_COLS)]

    def copy(i):
        src, _, c0 = chunks[i]
        return pltpu.make_async_copy(src.at[:, c0:c0 + WEIGHT_STAGE_COLS], slot(i), sem_ref.at[i % slots])

    for i in range(min(ahead, len(chunks))):
        copy(i).start()
    for i, (_, dst, c0) in enumerate(chunks):
        copy(i).wait()
        staged = slot(i)[...]
        if i + ahead < len(chunks):
            copy(i + ahead).start()
        dst[:, c0:c0 + WEIGHT_STAGE_COLS] = staged.astype(_BF16)


def _fused_kernel(xp_ref, xs_hbm, state_in_hbm, norm_w_ref, w_in_hbm, lb_logits_ref, g_norm_w_ref,
                  ln_v_w_ref, ln_v_b_ref, w_s_ref, b_s_ref, w_out_hbm, final_w_ref,
                  yp_ref, state_p_ref, ys_hbm, state_s_hbm, v_hbm,
                  state_t_scr, state_prev_scr, w_in_ref, w_out_ref, stage_scr, stage_sem,
                  xs_ref, state_in_ref, ys_ref, state_s_ref, v_ref, io_sem,
                  *, sample_step, last_step, steps_per_stream, n_streams, frames):
    step = pl.program_id(0)
    refs = (norm_w_ref, w_in_ref, lb_logits_ref, g_norm_w_ref, ln_v_w_ref, ln_v_b_ref,
            w_s_ref, b_s_ref, w_out_ref, final_w_ref)
    sample_in = [pltpu.make_async_copy(src, dst, io_sem.at[i])
                 for i, (src, dst) in enumerate([(xs_hbm, xs_ref), (state_in_hbm, state_in_ref)])]
    sample_out = [pltpu.make_async_copy(src, dst, io_sem.at[len(sample_in) + i])
                  for i, (src, dst) in enumerate([(ys_ref, ys_hbm), (state_s_ref, state_s_hbm), (v_ref, v_hbm)])]

    @pl.when(step == 0)
    def _():
        _stage_weights_bf16([(w_in_hbm, w_in_ref), (w_out_hbm, w_out_ref)], stage_scr, stage_sem)
        for copy in sample_in:
            copy.start()

    def too_steep(low):
        return jnp.logical_not(low >= -LOG_DECAY_LIMIT)

    @pl.when(step != sample_step)
    def _prompt():
        j = lax.rem(jnp.where(step > sample_step, step - 1, step), steps_per_stream)

        @pl.when(j == 0)
        def _():
            state_t_scr[...] = jnp.zeros_like(state_t_scr)

        state_prev_scr[...] = state_t_scr[...]
        states = [state_t_scr[0, hd] for hd in range(H_A)]
        low = None
        for blk in range(BLOCKS_PER_STEP):
            rows = slice(blk * PROMPT_BLOCK, (blk + 1) * PROMPT_BLOCK)
            fast = functools.partial(_recurrence_fast, states_t=[states], group=PROMPT_GROUP,
                                     chunk=PROMPT_CHUNK, carry_state=True)
            y, _, (new_states, low_blk) = _layer_block(xp_ref[0, rows, :], refs, fast,
                                                       group=PROMPT_GROUP, mix_chunk=MLP_CHUNK)
            yp_ref[0, rows, :] = y
            states = new_states[-1]
            low = low_blk if low is None else jnp.minimum(low, low_blk)
        for hd in range(H_A):
            state_t_scr[0, hd] = states[hd]

        @pl.when(too_steep(low))
        def _():
            state_t_scr[...] = state_prev_scr[...]
            exact = functools.partial(_recurrence_exact, work_ref=stage_scr, state_ref=state_t_scr,
                                      group=PROMPT_GROUP, chunk=PROMPT_BLOCK)
            for blk in range(BLOCKS_PER_STEP):
                rows = slice(blk * PROMPT_BLOCK, (blk + 1) * PROMPT_BLOCK)
                y_exact, _, _ = _layer_block(xp_ref[0, rows, :], refs, exact,
                                             group=PROMPT_GROUP, mix_chunk=MLP_CHUNK)
                yp_ref[0, rows, :] = y_exact

        @pl.when(j == steps_per_stream - 1)
        def _():
            for hd in range(H_A):
                state_p_ref[0, hd] = state_t_scr[0, hd].T

    @pl.when(step == sample_step)
    def _sample():
        for copy in sample_in:
            copy.wait()
        rows = n_streams * frames
        fast = functools.partial(
            _recurrence_fast,
            states_t=[[state_in_ref[s, hd] for hd in range(H_A)] for s in range(n_streams)],
            group=rows, chunk=frames, carry_state=False)
        y, v_norm, (new_states, low) = _layer_block(xs_ref[...], refs, fast, group=rows, mix_chunk=frames)
        ys_ref[...] = y
        v_ref[...] = v_norm
        for s in range(n_streams):
            for hd in range(H_A):
                state_s_ref[s, hd] = new_states[s][hd]

        @pl.when(too_steep(low))
        def _():
            for s in range(n_streams):
                for hd in range(H_A):
                    state_s_ref[s, hd] = state_in_ref[s, hd].T
            exact = functools.partial(_recurrence_exact, work_ref=stage_scr, state_ref=state_s_ref,
                                      group=rows, chunk=frames)
            y_exact, v_exact, _ = _layer_block(xs_ref[...], refs, exact, group=rows, mix_chunk=frames)
            ys_ref[...] = y_exact
            v_ref[...] = v_exact
            for s in range(n_streams):
                for hd in range(H_A):
                    state_s_ref[s, hd] = state_s_ref[s, hd].T

        for copy in sample_out:
            copy.start()

    @pl.when(step == last_step)
    def _():
        for copy in sample_out:
            copy.wait()


def _full(shape):
    return pl.BlockSpec(shape, lambda *_: (0,) * len(shape))


def kernel(x_prompt, x_sample, state_hgrn, norm_w, w_in, lb_logits, g_norm_w, ln_v_w, ln_v_b, w_s, b_s, w_out, final_norm_w):
    batch, seq, _ = x_prompt.shape
    dec_batch, dec_seq, _ = x_sample.shape
    assert norm_w.shape[0] == 1, "single-layer trunk"
    assert seq % (PROMPT_BLOCK * BLOCKS_PER_STEP) == 0 and PROMPT_BLOCK % PROMPT_GROUP == 0
    assert PROMPT_GROUP % MLP_CHUNK == 0 and MLP_CHUNK % PROMPT_CHUNK == 0
    assert dec_seq <= HGRN_CHUNK and dec_seq <= MLP_CHUNK and dec_seq & (dec_seq - 1) == 0
    assert w_s.shape[2:] == (MLP_CHUNK, MLP_CHUNK)

    rp = PROMPT_BLOCK * BLOCKS_PER_STEP
    rs = dec_batch * dec_seq
    steps_per_stream = seq // rp
    prompt_steps = batch * steps_per_stream
    row = lambda a: a.reshape(1, -1)

    sample_step = prompt_steps // 2

    def x_map(step):
        blk = jnp.where(step > sample_step, step - 1, step)
        return blk // steps_per_stream, blk % steps_per_stream, 0

    def y_map(step):
        blk = jnp.where(step >= sample_step, step - 1, step)
        return blk // steps_per_stream, blk % steps_per_stream, 0

    def state_map(step):
        return y_map(step)[0], 0, 0, 0

    any_space = pl.BlockSpec(memory_space=pl.ANY)

    y_prompt, state_p, y_sample, state_s, v_sample = pl.pallas_call(
        functools.partial(_fused_kernel, sample_step=sample_step, last_step=prompt_steps,
                          steps_per_stream=steps_per_stream, n_streams=dec_batch, frames=dec_seq),
        grid=(prompt_steps + 1,),
        in_specs=[pl.BlockSpec((1, rp, D_MODEL), x_map), any_space, any_space,
                  _full((1, D_MODEL)), any_space, _full(lb_logits.shape),
                  _full((1, D_A)), _full((1, D_B)), _full((1, D_B)),
                  _full((H_B, MLP_CHUNK, MLP_CHUNK)), _full((H_B, MLP_CHUNK)),
                  any_space, _full((1, D_MODEL))],
        out_specs=[pl.BlockSpec((1, rp, D_MODEL), y_map),
                   pl.BlockSpec((1, H_A, DK, DV), state_map),
                   any_space, any_space, any_space],
        out_shape=[jax.ShapeDtypeStruct((batch, seq, D_MODEL), _F32),
                   jax.ShapeDtypeStruct((batch, H_A, DK, DV), _F32),
                   jax.ShapeDtypeStruct((rs, D_MODEL), _F32),
                   jax.ShapeDtypeStruct((dec_batch, H_A, DK, DV), _F32),
                   jax.ShapeDtypeStruct((rs, D_B), _F32)],
        scratch_shapes=[pltpu.VMEM((1, H_A, DV, DK), _F32),
                        pltpu.VMEM((1, H_A, DV, DK), _F32),
                        pltpu.VMEM((D_MODEL, IN_WIDTH), _BF16),
                        pltpu.VMEM((D_MODEL, D_MODEL), _BF16),
                        pltpu.VMEM((WEIGHT_STAGE_SLABS, D_MODEL, D_A), _F32),
                        pltpu.SemaphoreType.DMA((WEIGHT_STAGE_SLABS * (D_A // WEIGHT_STAGE_COLS),)),
                        pltpu.VMEM((rs, D_MODEL), _F32),
                        pltpu.VMEM((dec_batch, H_A, DK, DV), _F32),
                        pltpu.VMEM((rs, D_MODEL), _F32),
                        pltpu.VMEM((dec_batch, H_A, DK, DV), _F32),
                        pltpu.VMEM((rs, D_B), _F32),
                        pltpu.SemaphoreType.DMA((5,))],
        compiler_params=pltpu.CompilerParams(
            dimension_semantics=("arbitrary",), vmem_limit_bytes=VMEM_LIMIT_BYTES),
        name="layer_step",
    )(x_prompt, x_sample.reshape(rs, D_MODEL), state_hgrn[0],
      row(norm_w[0]), w_in.reshape(D_MODEL, IN_WIDTH), lb_logits, row(g_norm_w[0]), row(ln_v_w[0]), row(ln_v_b[0]),
      w_s[0], b_s[0], w_out.reshape(D_MODEL, D_MODEL), row(final_norm_w))

    return (y_prompt,
            y_sample.reshape(dec_batch, dec_seq, D_MODEL),
            state_p[None],
            state_s[None],
            v_sample.reshape(1, dec_batch, dec_seq, D_B))
```
